```python
import jax, jax.numpy as jnp
from jax import lax
import numpy as np

D_MODEL = 1024
BATCH = 8
SEQ = 2048
DEPTH = 4

CHUNK = 64
N_MIXERS = 3
N_GLA = (DEPTH + 2) // 3
N_MLA = (DEPTH + 1) // 3
N_CONV = DEPTH // 3
ALPHA = (2 * DEPTH) ** 0.25
BETA = (8 * DEPTH) ** -0.25
LN_EPS = 1e-5
RMS_EPS = 1e-6
PLE_DIM = 256
D_FF = 4 * D_MODEL
MAX_OFFSET = 4096

GLA_HEADS = 4
GLA_DK = D_MODEL // 2 // GLA_HEADS
GLA_DV = D_MODEL // GLA_HEADS
GLA_GATE_RANK = 16
GLA_TAU = 16.0
GLA_HK = GLA_HEADS * GLA_DK
GLA_HV = GLA_HEADS * GLA_DV
GLA_SPLITS = [GLA_HK, 2 * GLA_HK, 2 * GLA_HK + GLA_HV, 2 * GLA_HK + GLA_HV + D_MODEL]
GLA_IN = 2 * GLA_HK + GLA_HV + D_MODEL + GLA_GATE_RANK

MLA_HEADS = 8
MLA_NOPE = 128
MLA_ROPE = 64
MLA_V = 128
MLA_Q_RANK = 256
MLA_KV_RANK = 256
MLA_IN = MLA_Q_RANK + MLA_KV_RANK + MLA_ROPE
ROPE_BASE = 10000.0
Q_BLOCK = 128

CONV_WIDTH = 3

kernel_name = 'hybrid_gla_mla_shortconv_deepnorm_trunk'


def layer_norm(x, g, b):
    xf = x.astype(jnp.float32)
    mu = jnp.mean(xf, -1, keepdims=True)
    var = jnp.mean(jnp.square(xf - mu), -1, keepdims=True)
    return ((xf - mu) * lax.rsqrt(var + LN_EPS) * g + b).astype(x.dtype)


def rms_norm(x, g):
    xf = x.astype(jnp.float32)
    return (xf * lax.rsqrt(jnp.mean(xf * xf, -1, keepdims=True) + RMS_EPS) * g).astype(x.dtype)


def rope(x, cos, sin):
    x1, x2 = jnp.split(x, 2, axis=-1)
    return jnp.concatenate([x1 * cos - x2 * sin, x2 * cos + x1 * sin], axis=-1)


def gla_mixer(x, w_in, w_gate_up, b_gate, norm_g, w_out):
    B_, S_, _ = x.shape
    nc = S_ // CHUNK
    q, k, v, r, g_lr = jnp.split(x @ w_in, GLA_SPLITS, axis=-1)
    log_a = jax.nn.log_sigmoid((g_lr @ w_gate_up + b_gate).astype(jnp.float32)) / GLA_TAU

    def to_chunks(t, d):
        return t.astype(jnp.float32).reshape(B_, nc, CHUNK, GLA_HEADS, d).transpose(1, 0, 3, 2, 4)

    qc = to_chunks(q, GLA_DK) * GLA_DK ** -0.5
    kc = to_chunks(k, GLA_DK)
    vc = to_chunks(v, GLA_DV)
    lc = to_chunks(log_a, GLA_DK)

    def step(state, inp):
        q_, k_, v_, la = inp
        L = jnp.cumsum(la, axis=2)
        decay = jnp.exp(-jnp.abs(L[:, :, :, None, :] - L[:, :, None, :, :]))
        scores = jnp.einsum('bhtd,bhsd,bhtsd->bhts', q_, k_, decay)
        o = scores @ v_ + (q_ * jnp.exp(L)) @ state
        L_end = L[:, :, -1:, :]
        state = (jnp.exp(L_end[:, :, 0, :, None]) * state
                 + jnp.einsum('bhsd,bhse->bhde', k_ * jnp.exp(L_end - L), v_))
        return state, o

    s0 = jnp.zeros((B_, GLA_HEADS, GLA_DK, GLA_DV), jnp.float32)
    _, o = lax.scan(step, s0, (qc, kc, vc, lc))
    o = o.transpose(1, 0, 3, 2, 4).reshape(B_, S_, GLA_HEADS, GLA_DV)
    o = rms_norm(o, norm_g).reshape(B_, S_, GLA_HV) * jax.nn.silu(r.astype(jnp.float32))
    return o.astype(x.dtype) @ w_out


def mla_mixer(x, cos, sin, w_in, q_norm, kv_norm, w_uq, w_ukv, w_out):
    B_, S_, _ = x.shape
    c_q, c_kv, k_rope = jnp.split(x @ w_in, [MLA_Q_RANK, MLA_Q_RANK + MLA_KV_RANK], axis=-1)
    q = (rms_norm(c_q, q_norm) @ w_uq).reshape(B_, S_, MLA_HEADS, MLA_NOPE + MLA_ROPE)
    kv = (rms_norm(c_kv, kv_norm) @ w_ukv).reshape(B_, S_, MLA_HEADS, MLA_NOPE + MLA_V)
    q_nope, q_rope = jnp.split(q, [MLA_NOPE], axis=-1)
    k_nope, v = jnp.split(kv, [MLA_NOPE], axis=-1)
    q_rope = rope(q_rope, cos[:, :, None, :], sin[:, :, None, :])
    k_rope = rope(k_rope, cos, sin)
    qf = jnp.concatenate([q_nope.astype(jnp.float32), q_rope.astype(jnp.float32)], axis=-1)
    qf = qf * (MLA_NOPE + MLA_ROPE) ** -0.5
    kf = jnp.concatenate([k_nope.astype(jnp.float32),
                          jnp.broadcast_to(k_rope.astype(jnp.float32)[:, :, None, :],
                                           (B_, S_, MLA_HEADS, MLA_ROPE))], axis=-1)
    n_qb = S_ // Q_BLOCK
    q_blocks = qf.reshape(B_, n_qb, Q_BLOCK, MLA_HEADS, MLA_NOPE + MLA_ROPE).transpose(1, 0, 2, 3, 4)
    key_chunk = jnp.arange(S_) // CHUNK

    def attend(args):
        qb, bi = args
        q_chunk = (bi * Q_BLOCK + jnp.arange(Q_BLOCK)) // CHUNK
        s = jnp.einsum('bqhd,bkhd->bhqk', qb, kf)
        s = jnp.where(key_chunk[None, :] <= q_chunk[:, None], s, -jnp.inf)
        pr = jax.nn.softmax(s, axis=-1)
        return jnp.einsum('bhqk,bkhd->bqhd', pr.astype(v.dtype), v)

    o = lax.map(attend, (q_blocks, jnp.arange(n_qb)))
    o = o.transpose(1, 0, 2, 3, 4).reshape(B_, S_, MLA_HEADS * MLA_V)
    return o.astype(x.dtype) @ w_out


def conv_mixer(x, w_in, conv_w, w_out):
    b, c, u = jnp.split(x @ w_in, 3, axis=-1)
    z = lax.conv_general_dilated(c * u, conv_w[:, None, :], window_strides=(1,),
                                 padding=[(CONV_WIDTH - 1, 0)],
                                 dimension_numbers=('NWC', 'WIO', 'NWC'),
                                 feature_group_count=D_MODEL)
    return (b * z) @ w_out


def sq_relu_mlp(x, w1, w2):
    return jnp.square(jax.nn.relu(x @ w1)) @ w2


def setup_inputs(seed: int = 0) -> dict:
    key = jax.random.key(seed)
    ks = jax.random.split(key, 24)

    def nrm(i, shape, scale):
        return jax.random.normal(ks[i], shape, jnp.float32) * scale

    x = nrm(0, (BATCH, SEQ, D_MODEL), 1.0)
    p = nrm(1, (DEPTH, BATCH, SEQ, PLE_DIM), 1.0)
    offsets = jax.random.randint(ks[2], (BATCH, 1), 0, MAX_OFFSET, dtype=jnp.int32)
    positions = (offsets + jnp.arange(SEQ, dtype=jnp.int32)[None, :]).astype(jnp.int32)
    return {
        'x': x,
        'p': p,
        'positions': positions,
        'gla_w_in': nrm(3, (N_GLA, D_MODEL, GLA_IN), D_MODEL ** -0.5),
        'gla_w_gate_up': nrm(4, (N_GLA, GLA_GATE_RANK, GLA_HK), GLA_GATE_RANK ** -0.5),
        'gla_b_gate': nrm(5, (N_GLA, GLA_HK), 0.1),
        'gla_norm_g': 1.0 + nrm(6, (N_GLA, GLA_DV), 0.01),
        'gla_w_out': nrm(7, (N_GLA, GLA_HV, D_MODEL), GLA_HV ** -0.5 * BETA),
        'mla_w_in': nrm(8, (N_MLA, D_MODEL, MLA_IN), D_MODEL ** -0.5),
        'mla_q_norm': 1.0 + nrm(9, (N_MLA, MLA_Q_RANK), 0.01),
        'mla_kv_norm': 1.0 + nrm(10, (N_MLA, MLA_KV_RANK), 0.01),
        'mla_w_uq': nrm(11, (N_MLA, MLA_Q_RANK, MLA_HEADS * (MLA_NOPE + MLA_ROPE)), MLA_Q_RANK ** -0.5),
        'mla_w_ukv': nrm(12, (N_MLA, MLA_KV_RANK, MLA_HEADS * (MLA_NOPE + MLA_V)), MLA_KV_RANK ** -0.5),
        'mla_w_out': nrm(13, (N_MLA, MLA_HEADS * MLA_V, D_MODEL), (MLA_HEADS * MLA_V) ** -0.5 * BETA),
        'conv_w_in': nrm(14, (N_CONV, D_MODEL, 3 * D_MODEL), D_MODEL ** -0.5),
        'conv_w': nrm(15, (N_CONV, CONV_WIDTH, D_MODEL), CONV_WIDTH ** -0.5),
        'conv_w_out': nrm(16, (N_CONV, D_MODEL, D_MODEL), D_MODEL ** -0.5 * BETA),
        'ln_g': 1.0 + nrm(17, (DEPTH, 2, D_MODEL), 0.01),
        'ln_b': nrm(18, (DEPTH, 2, D_MODEL), 0.01),
        'mlp_w1': nrm(19, (DEPTH, D_MODEL, D_FF), D_MODEL ** -0.5),
        'mlp_w2': nrm(20, (DEPTH, D_FF, D_MODEL), D_FF ** -0.5 * BETA),
        'ple_w_gate': nrm(21, (DEPTH, D_MODEL, D_MODEL), D_MODEL ** -0.5),
        'ple_w_proj': nrm(22, (DEPTH, PLE_DIM, D_MODEL), PLE_DIM ** -0.5),
    }


def reference(x, p, positions, gla_w_in, gla_w_gate_up, gla_b_gate, gla_norm_g, gla_w_out,
              mla_w_in, mla_q_norm, mla_kv_norm, mla_w_uq, mla_w_ukv, mla_w_out,
              conv_w_in, conv_w, conv_w_out, ln_g, ln_b, mlp_w1, mlp_w2,
              ple_w_gate, ple_w_proj):
    inv_freq = ROPE_BASE ** (-jnp.arange(0, MLA_ROPE // 2, dtype=jnp.float32) * (2.0 / MLA_ROPE))
    ang = positions.astype(jnp.float32)[..., None] * inv_freq
    cos, sin = jnp.cos(ang), jnp.sin(ang)
    for i in range(DEPTH):
        j = i // N_MIXERS
        kind = i % N_MIXERS
        if kind == 0:
            h = gla_mixer(x, gla_w_in[j], gla_w_gate_up[j], gla_b_gate[j], gla_norm_g[j], gla_w_out[j])
        elif kind == 1:
            h = mla_mixer(x, cos, sin, mla_w_in[j], mla_q_norm[j], mla_kv_norm[j],
                          mla_w_uq[j], mla_w_ukv[j], mla_w_out[j])
        else:
            h = conv_mixer(x, conv_w_in[j], conv_w[j], conv_w_out[j])
        x = layer_norm(ALPHA * x + h, ln_g[i, 0], ln_b[i, 0])
        x = layer_norm(ALPHA * x + sq_relu_mlp(x, mlp_w1[i], mlp_w2[i]), ln_g[i, 1], ln_b[i, 1])
        x = x + jax.nn.sigmoid(x @ ple_w_gate[i]) * (p[i] @ ple_w_proj[i])
    return x
```

```python
import functools

import numpy as np
import jax
import jax.numpy as jnp
from jax.experimental import pallas as pl
from jax.experimental.pallas import tpu as pltpu

F32 = jnp.float32
BF16 = jnp.bfloat16

D_MODEL = 1024
DEPTH = 4
CHUNK = 64
N_MIXERS = 3
ALPHA = (2 * DEPTH) ** 0.25
LN_EPS = 1e-5
RMS_EPS = 1e-6
PLE_DIM = 256
D_FF = 4 * D_MODEL

GLA_HEADS = 4
GLA_DK = 128
GLA_DV = 256
GLA_GATE_RANK = 16
GLA_TAU = 16.0
GLA_HK = GLA_HEADS * GLA_DK
GLA_HV = GLA_HEADS * GLA_DV
GLA_SUB = 16
SUB_SHIFT = GLA_SUB.bit_length() - 1
CHUNK_SHIFT = CHUNK.bit_length() - 1
assert 1 << SUB_SHIFT == GLA_SUB and 1 << CHUNK_SHIFT == CHUNK

MLA_HEADS = 8
MLA_NOPE = 128
MLA_ROPE = 64
MLA_V = 128
MLA_Q_RANK = 256
MLA_KV_RANK = 256
MLA_QK_PAD = 256
ROPE_BASE = 10000.0

CONV_WIDTH = 3

LANES = 128
SUBLANES = 8
VMEM_LIMIT_BYTES = 56 * 1024 * 1024

TAIL_ROWS = 512
FF_CHUNK = 1024
GLA_PROJ_ROWS = 256
GLA_SCAN_ROWS = 256
MLA_PROJ_ROWS = 512
MLA_Q_TILE = 256
CONV_ROWS = 512


def _params(*semantics):
    return pltpu.CompilerParams(dimension_semantics=semantics, vmem_limit_bytes=VMEM_LIMIT_BYTES)


def _const_spec(shape):
    return pl.BlockSpec(shape, lambda *_: (0,) * len(shape), pipeline_mode=pl.Buffered(1))


def _dot(a, b):
    return jnp.dot(a, b, preferred_element_type=F32)


def _dot_nt(a, b):
    return jax.lax.dot_general(a, b, (((1,), (1,)), ((), ())), preferred_element_type=F32)


def _dot_tn(a, b):
    return jax.lax.dot_general(a, b, (((0,), (0,)), ((), ())), preferred_element_type=F32)


def _layer_norm(x, g, b):
    mu = jnp.mean(x, axis=-1, keepdims=True)
    xc = x - mu
    var = jnp.mean(xc * xc, axis=-1, keepdims=True)
    return xc * jax.lax.rsqrt(var + LN_EPS) * g + b


def _rms_norm(x, g):
    return x * jax.lax.rsqrt(jnp.mean(x * x, axis=-1, keepdims=True) + RMS_EPS) * g


def _sigmoid(x):
    return 1.0 / (1.0 + jnp.exp(-x))


def _tail_kernel(o_ref, x_ref, p_ref, wout_ref, w1_ref, w2_ref, wg_ref, wp_ref, ln_ref, out_ref):
    g0, g1 = ln_ref[0:1, :], ln_ref[1:2, :]
    b0, b1 = ln_ref[2:3, :], ln_ref[3:4, :]
    x = x_ref[...]
    h = _dot(o_ref[...], wout_ref[...])
    x1 = _layer_norm(ALPHA * x + h, g0, b0)
    x1b = x1.astype(BF16)
    acc = jnp.zeros_like(x1)
    for c in range(D_FF // FF_CHUNK):
        cols = slice(c * FF_CHUNK, (c + 1) * FF_CHUNK)
        hc = _dot(x1b, w1_ref[:, cols])
        hc = jnp.square(jnp.maximum(hc, 0.0)).astype(BF16)
        acc = acc + _dot(hc, w2_ref[cols, :])
    x2 = _layer_norm(ALPHA * x1 + acc, g1, b1)
    gate = _sigmoid(_dot(x2.astype(BF16), wg_ref[...]))
    proj = _dot(p_ref[...].astype(BF16), wp_ref[...])
    out_ref[...] = x2 + gate * proj


def _tail(o, x, p, wout, w1, w2, wg, wp, ln):
    n = x.shape[0]
    tm = TAIL_ROWS
    rows = lambda w: pl.BlockSpec((tm, w), lambda i: (i, 0))
    return pl.pallas_call(
        _tail_kernel,
        grid=(n // tm,),
        in_specs=[rows(D_MODEL), rows(D_MODEL), rows(PLE_DIM),
                  _const_spec(wout.shape), _const_spec(w1.shape), _const_spec(w2.shape),
                  _const_spec(wg.shape), _const_spec(wp.shape), _const_spec(ln.shape)],
        out_specs=rows(D_MODEL),
        out_shape=jax.ShapeDtypeStruct((n, D_MODEL), F32),
        compiler_params=_params("parallel"),
        name="tail",
    )(o, x, p, wout, w1, w2, wg, wp, ln)


def _split3_bf16(a):
    hi = a.astype(BF16)
    r1 = a - hi.astype(F32)
    mid = r1.astype(BF16)
    lo = (r1 - mid.astype(F32)).astype(BF16)
    return hi, mid, lo


def _gla_proj_kernel(x_ref, wqkvr_ref, wlr_ref, wgu_ref, bg_ref, tri_ref,
                     q_ref, k_ref, v_ref, r_ref, l_ref):
    xb = x_ref[...].astype(BF16)
    q_ref[...] = _dot(xb, wqkvr_ref[:, 0:GLA_HK]) * (GLA_DK ** -0.5)
    k_ref[...] = _dot(xb, wqkvr_ref[:, GLA_HK:2 * GLA_HK])
    v_ref[...] = _dot(xb, wqkvr_ref[:, 2 * GLA_HK:2 * GLA_HK + GLA_HV]).astype(BF16)
    r = _dot(xb, wqkvr_ref[:, 2 * GLA_HK + GLA_HV:])
    r_ref[...] = r * _sigmoid(r)
    g_lr = _dot(xb, wlr_ref[...])
    z = _dot(g_lr.astype(BF16), wgu_ref[...]) + bg_ref[...]
    log_a = (jnp.minimum(z, 0.0) - jnp.log1p(jnp.exp(-jnp.abs(z)))) * (1.0 / GLA_TAU)
    tri = tri_ref[...]
    hi, mid, lo = _split3_bf16(log_a)
    l_ref[...] = (_dot(tri, lo) + _dot(tri, mid)) + _dot(tri, hi)


def _gla_proj(x, wqkvr, wlr, wgu, bg, tri):
    n = x.shape[0]
    tm = GLA_PROJ_ROWS
    rows = lambda w: pl.BlockSpec((tm, w), lambda i: (i, 0))
    return pl.pallas_call(
        _gla_proj_kernel,
        grid=(n // tm,),
        in_specs=[rows(D_MODEL), _const_spec(wqkvr.shape), _const_spec(wlr.shape),
                  _const_spec(wgu.shape), _const_spec(bg.shape), _const_spec(tri.shape)],
        out_specs=[rows(GLA_HK), rows(GLA_HK), rows(GLA_HV), rows(GLA_HV), rows(GLA_HK)],
        out_shape=[jax.ShapeDtypeStruct((n, GLA_HK), F32), jax.ShapeDtypeStruct((n, GLA_HK), F32),
                   jax.ShapeDtypeStruct((n, GLA_HV), BF16), jax.ShapeDtypeStruct((n, GLA_HV), F32),
                   jax.ShapeDtypeStruct((n, GLA_HK), F32)],
        compiler_params=_params("parallel"),
        name="gla_proj",
    )(x, wqkvr, wlr, wgu, bg, tri)


def _gla_chunk_head(q, k, v, L, state_t):
    C, nsub = CHUNK, CHUNK // GLA_SUB
    l_end = L[C - 1:C, :]
    o = _dot_nt((q * jnp.exp(L)).astype(BF16), state_t.astype(BF16))
    k_dec = (k * jnp.exp(l_end - L)).astype(BF16)
    new_state_t = state_t * jnp.exp(l_end) + _dot_tn(v, k_dec)

    row = jax.lax.broadcasted_iota(jnp.int32, (C, GLA_DK), 0)
    l_start = jnp.concatenate(
        [jnp.broadcast_to(L[i * GLA_SUB - 1:i * GLA_SUB, :] if i else jnp.zeros_like(l_end),
                          (GLA_SUB, GLA_DK)) for i in range(nsub)], axis=0)
    after = jnp.exp(L - l_start)
    row_sub = jnp.right_shift(row, SUB_SHIFT)
    q_parts, k_parts = [], []
    for i in range(1, nsub):
        l_b = L[i * GLA_SUB - 1:i * GLA_SUB, :]
        before = jnp.exp(jnp.minimum(l_b - L, 0.0))
        is_before, is_in = row_sub < i, row_sub == i
        q_parts.append(jnp.where(is_in, q * after, 0.0))
        k_parts.append(jnp.where(is_before, k * before, 0.0))
        q_parts.append(jnp.where(is_before, q * before, 0.0))
        k_parts.append(jnp.where(is_in, k * after, 0.0))
    q_cat = jnp.concatenate(q_parts, axis=1).astype(BF16)
    k_cat = jnp.concatenate(k_parts, axis=1).astype(BF16)
    scores = _dot_nt(q_cat, k_cat)

    r2 = jax.lax.broadcasted_iota(jnp.int32, (C, C), 0)
    c2 = jax.lax.broadcasted_iota(jnp.int32, (C, C), 1)
    offset = c2 - jnp.left_shift(jnp.right_shift(r2, SUB_SHIFT), SUB_SHIFT)
    for j in range(GLA_SUB):
        l_s = jnp.concatenate(
            [jnp.broadcast_to(L[i * GLA_SUB + j:i * GLA_SUB + j + 1, :], (GLA_SUB, GLA_DK))
             for i in range(nsub)], axis=0)
        k_s = jnp.concatenate(
            [jnp.broadcast_to(k[i * GLA_SUB + j:i * GLA_SUB + j + 1, :], (GLA_SUB, GLA_DK))
             for i in range(nsub)], axis=0)
        w = q * k_s * jnp.exp(-jnp.abs(L - l_s))
        scores = jnp.where(offset == j, jnp.sum(w, axis=1, keepdims=True), scores)

    o = o + _dot(scores.astype(BF16), v)
    return o, new_state_t


def _gla_scan_kernel(q_ref, k_ref, v_ref, r_ref, l_ref, g_ref, o_ref, state_ref):
    @pl.when(pl.program_id(1) == 0)
    def _():
        state_ref[...] = jnp.zeros_like(state_ref)

    g = g_ref[...]

    def chunk_body(c, carry):
        rows = pl.ds(pl.multiple_of(c * CHUNK, CHUNK), CHUNK)
        for h in range(GLA_HEADS):
            kc = slice(h * GLA_DK, (h + 1) * GLA_DK)
            vc = slice(h * GLA_DV, (h + 1) * GLA_DV)
            o, new_state = _gla_chunk_head(q_ref[rows, kc], k_ref[rows, kc], v_ref[rows, vc],
                                           l_ref[rows, kc], state_ref[h])
            state_ref[h] = new_state
            o_ref[rows, vc] = (_rms_norm(o, g) * r_ref[rows, vc]).astype(BF16)
        return carry

    jax.lax.fori_loop(0, GLA_SCAN_ROWS // CHUNK, chunk_body, 0)


def _gla_scan(q, k, v, r, L, g, batch, seq):
    n = q.shape[0]
    t = GLA_SCAN_ROWS
    per_b = seq // t
    rows = lambda w: pl.BlockSpec((t, w), lambda b, j: (b * per_b + j, 0))
    return pl.pallas_call(
        _gla_scan_kernel,
        grid=(batch, per_b),
        in_specs=[rows(GLA_HK), rows(GLA_HK), rows(GLA_HV), rows(GLA_HV), rows(GLA_HK),
                  _const_spec(g.shape)],
        out_specs=rows(GLA_HV),
        out_shape=jax.ShapeDtypeStruct((n, GLA_HV), BF16),
        scratch_shapes=[pltpu.VMEM((GLA_HEADS, GLA_DV, GLA_DK), F32)],
        compiler_params=_params("parallel", "arbitrary"),
        name="gla_scan",
    )(q, k, v, r, L, g)


def _gla_front(x, w_in, w_gate_up, b_gate, norm_g, batch, seq):
    wqkvr = w_in[:, :2 * GLA_HK + GLA_HV + D_MODEL].astype(BF16)
    wlr = jnp.pad(w_in[:, 2 * GLA_HK + GLA_HV + D_MODEL:], ((0, 0), (0, LANES - GLA_GATE_RANK))).astype(BF16)
    wgu = jnp.pad(w_gate_up, ((0, LANES - GLA_GATE_RANK), (0, 0))).astype(BF16)
    idx = np.arange(GLA_PROJ_ROWS)
    tri = ((idx[:, None] // CHUNK == idx[None, :] // CHUNK) & (idx[None, :] <= idx[:, None]))
    tri = jnp.asarray(tri, BF16)
    q, k, v, r, L = _gla_proj(x, wqkvr, wlr, wgu, b_gate[None, :], tri)
    return _gla_scan(q, k, v, r, L, norm_g[None, :], batch, seq)


def _mla_proj_kernel(x_ref, pos_ref, freq_ref, sign_ref, win_ref, qn_ref, kvn_ref,
                     wq_ref, wkv_ref, q_ref, k_ref, v_ref):
    H, R = MLA_HEADS, MLA_Q_RANK
    xb = x_ref[...].astype(BF16)
    c = _dot(xb, win_ref[...])
    cq = _rms_norm(c[:, 0:R], qn_ref[...]).astype(BF16)
    ckv = _rms_norm(c[:, R:R + MLA_KV_RANK], kvn_ref[...]).astype(BF16)
    kr = c[:, 2 * R:2 * R + LANES]
    kr_sw = c[:, 2 * R + LANES:2 * R + 2 * LANES]
    ang = pos_ref[...] * freq_ref[...]
    cos = jnp.cos(ang)
    sin = jnp.sin(ang) * sign_ref[...]
    k_rope = (kr * cos + kr_sw * sin).astype(BF16)
    qq = _dot(cq, wq_ref[...])
    kv = _dot(ckv, wkv_ref[...])
    scale = (MLA_NOPE + MLA_ROPE) ** -0.5
    for h in range(H):
        lo = h * LANES
        q_nope = qq[:, lo:lo + LANES]
        q_r = qq[:, H * LANES + lo:H * LANES + lo + LANES]
        q_sw = qq[:, 2 * H * LANES + lo:2 * H * LANES + lo + LANES]
        base = h * MLA_QK_PAD
        q_ref[:, base:base + LANES] = (q_nope * scale).astype(BF16)
        q_ref[:, base + LANES:base + 2 * LANES] = ((q_r * cos + q_sw * sin) * scale).astype(BF16)
        k_ref[:, base:base + LANES] = kv[:, lo:lo + LANES].astype(BF16)
        k_ref[:, base + LANES:base + 2 * LANES] = k_rope
    v_ref[...] = kv[:, H * LANES:].astype(BF16)


def _mla_proj(x, pos, freq, sign, win, qn, kvn, wq, wkv):
    n = x.shape[0]
    tm = MLA_PROJ_ROWS
    rows = lambda w: pl.BlockSpec((tm, w), lambda i: (i, 0))
    qk_w = MLA_HEADS * MLA_QK_PAD
    return pl.pallas_call(
        _mla_proj_kernel,
        grid=(n // tm,),
        in_specs=[rows(D_MODEL), rows(1), _const_spec(freq.shape), _const_spec(sign.shape),
                  _const_spec(win.shape), _const_spec(qn.shape), _const_spec(kvn.shape),
                  _const_spec(wq.shape), _const_spec(wkv.shape)],
        out_specs=[rows(qk_w), rows(qk_w), rows(MLA_HEADS * MLA_V)],
        out_shape=[jax.ShapeDtypeStruct((n, qk_w), BF16), jax.ShapeDtypeStruct((n, qk_w), BF16),
                   jax.ShapeDtypeStruct((n, MLA_HEADS * MLA_V), BF16)],
        compiler_params=_params("parallel"),
        name="mla_proj",
    )(x, pos, freq, sign, win, qn, kvn, wq, wkv)


def _mla_attn_kernel(q_ref, k_ref, v_ref, o_ref):
    seq = q_ref.shape[0]
    tq = MLA_Q_TILE
    r = jax.lax.broadcasted_iota(jnp.int32, (tq, tq), 0)
    c = jax.lax.broadcasted_iota(jnp.int32, (tq, tq), 1)
    visible = jnp.right_shift(c, CHUNK_SHIFT) <= jnp.right_shift(r, CHUNK_SHIFT)
    for qi in range(seq // tq):
        q = q_ref[qi * tq:(qi + 1) * tq, :]
        s_diag = jnp.where(visible, _dot_nt(q, k_ref[qi * tq:(qi + 1) * tq, :]), -jnp.inf)
        m = jnp.max(s_diag, axis=1, keepdims=True)
        if qi:
            s_past = _dot_nt(q, k_ref[0:qi * tq, :])
            m = jnp.maximum(m, jnp.max(s_past, axis=1, keepdims=True))
        p = jnp.exp(s_diag - m)
        den = jnp.sum(p, axis=1, keepdims=True)
        acc = _dot(p.astype(BF16), v_ref[qi * tq:(qi + 1) * tq, :])
        if qi:
            p = jnp.exp(s_past - m)
            den = den + jnp.sum(p, axis=1, keepdims=True)
            acc = acc + _dot(p.astype(BF16), v_ref[0:qi * tq, :])
        o_ref[qi * tq:(qi + 1) * tq, :] = (acc / den).astype(BF16)


def _mla_attn(q, k, v, batch, seq):
    n = q.shape[0]
    blk = lambda w: pl.BlockSpec((seq, w), lambda b, h: (b, h))
    return pl.pallas_call(
        _mla_attn_kernel,
        grid=(batch, MLA_HEADS),
        in_specs=[blk(MLA_QK_PAD), blk(MLA_QK_PAD), blk(MLA_V)],
        out_specs=blk(MLA_V),
        out_shape=jax.ShapeDtypeStruct((n, MLA_HEADS * MLA_V), BF16),
        compiler_params=_params("parallel", "parallel"),
        name="mla_attn",
    )(q, k, v)


def _mla_front(x, pos, w_in, q_norm, kv_norm, w_uq, w_ukv, batch, seq):
    H, half = MLA_HEADS, MLA_ROPE // 2
    pad = LANES - MLA_ROPE

    def rope_cols(w, swap):
        a, b = w[..., :half], w[..., half:]
        if swap:
            a, b = b, a
        return jnp.concatenate([a, b, jnp.zeros(w.shape[:-1] + (pad,), w.dtype)], axis=-1)

    kr_w = w_in[:, MLA_Q_RANK + MLA_KV_RANK:]
    win = jnp.concatenate([w_in[:, :MLA_Q_RANK + MLA_KV_RANK], rope_cols(kr_w, False),
                           rope_cols(kr_w, True)], axis=1).astype(BF16)
    wq3 = w_uq.reshape(MLA_Q_RANK, H, MLA_NOPE + MLA_ROPE)
    wq = jnp.concatenate([wq3[:, :, :MLA_NOPE].reshape(MLA_Q_RANK, H * LANES),
                          rope_cols(wq3[:, :, MLA_NOPE:], False).reshape(MLA_Q_RANK, H * LANES),
                          rope_cols(wq3[:, :, MLA_NOPE:], True).reshape(MLA_Q_RANK, H * LANES)],
                         axis=1).astype(BF16)
    wkv3 = w_ukv.reshape(MLA_KV_RANK, H, MLA_NOPE + MLA_V)
    wkv = jnp.concatenate([wkv3[:, :, :MLA_NOPE].reshape(MLA_KV_RANK, H * MLA_NOPE),
                           wkv3[:, :, MLA_NOPE:].reshape(MLA_KV_RANK, H * MLA_V)], axis=1).astype(BF16)
    inv_freq = ROPE_BASE ** (-jnp.arange(0, MLA_ROPE // 2, dtype=F32) * (2.0 / MLA_ROPE))
    zeros = jnp.zeros((pad,), F32)
    freq = jnp.concatenate([inv_freq, inv_freq, zeros])[None, :]
    sign = jnp.concatenate([-jnp.ones((half,), F32), jnp.ones((half,), F32), zeros])[None, :]
    q, k, v = _mla_proj(x, pos, freq, sign, win, q_norm[None, :], kv_norm[None, :], wq, wkv)
    return _mla_attn(q, k, v, batch, seq)


def _conv_kernel(x_ref, win_ref, cw_ref, o_ref, cu_ref):
    ts = x_ref.shape[0]
    halo = SUBLANES

    @pl.when(pl.program_id(1) == 0)
    def _():
        cu_ref[0:halo, :] = jnp.zeros((halo, D_MODEL), F32)

    xb = x_ref[...].astype(BF16)
    b = _dot(xb, win_ref[:, 0:D_MODEL])
    cu = _dot(xb, win_ref[:, D_MODEL:2 * D_MODEL]) * _dot(xb, win_ref[:, 2 * D_MODEL:])
    cu_ref[halo:halo + ts, :] = cu
    z = (cw_ref[0:1, :] * cu_ref[halo - 2:halo - 2 + ts, :]
         + cw_ref[1:2, :] * cu_ref[halo - 1:halo - 1 + ts, :]
         + cw_ref[2:3, :] * cu)
    o_ref[...] = (b * z).astype(BF16)
    cu_ref[0:halo, :] = cu_ref[ts:ts + halo, :]


def _conv_front(x, w_in, conv_w, batch, seq):
    n = x.shape[0]
    ts = CONV_ROWS
    per_b = seq // ts
    rows = lambda w: pl.BlockSpec((ts, w), lambda b, j: (b * per_b + j, 0))
    win = w_in.astype(BF16)
    return pl.pallas_call(
        _conv_kernel,
        grid=(batch, per_b),
        in_specs=[rows(D_MODEL), _const_spec(win.shape), _const_spec(conv_w.shape)],
        out_specs=rows(D_MODEL),
        out_shape=jax.ShapeDtypeStruct((n, D_MODEL), BF16),
        scratch_shapes=[pltpu.VMEM((SUBLANES + ts, D_MODEL), F32)],
        compiler_params=_params("parallel", "arbitrary"),
        name="conv_front",
    )(x, win, conv_w)


def kernel(x, p, positions, gla_w_in, gla_w_gate_up, gla_b_gate, gla_norm_g, gla_w_out,
           mla_w_in, mla_q_norm, mla_kv_norm, mla_w_uq, mla_w_ukv, mla_w_out,
           conv_w_in, conv_w, conv_w_out, ln_g, ln_b, mlp_w1, mlp_w2, ple_w_gate, ple_w_proj):
    batch, seq, d = x.shape
    n = batch * seq
    assert d == D_MODEL and seq % GLA_SCAN_ROWS == 0 and seq % CONV_ROWS == 0 and seq % MLA_Q_TILE == 0
    xf = x.reshape(n, d)
    pos = positions.astype(F32).reshape(n, 1)
    for i in range(DEPTH):
        j, kind = i // N_MIXERS, i % N_MIXERS
        if kind == 0:
            o = _gla_front(xf, gla_w_in[j], gla_w_gate_up[j], gla_b_gate[j], gla_norm_g[j], batch, seq)
            w_out = gla_w_out[j]
        elif kind == 1:
            o = _mla_front(xf, pos, mla_w_in[j], mla_q_norm[j], mla_kv_norm[j], mla_w_uq[j],
                           mla_w_ukv[j], batch, seq)
            w_out = mla_w_out[j]
        else:
            o = _conv_front(xf, conv_w_in[j], conv_w[j], batch, seq)
            w_out = conv_w_out[j]
        ln = jnp.concatenate([ln_g[i], ln_b[i]], axis=0)
        xf = _tail(o, xf, p[i].reshape(n, PLE_DIM), w_out.astype(BF16), mlp_w1[i].astype(BF16),
                   mlp_w2[i].astype(BF16), ple_w_gate[i].astype(BF16), ple_w_proj[i].astype(BF16), ln)
    return xf.reshape(batch, seq, d)
```

```python
import functools

import numpy as np
import jax
import jax.numpy as jnp
from jax.experimental import pallas as pl
from jax.experimental.pallas import tpu as pltpu

F32 = jnp.float32
BF16 = jnp.bfloat16

D_MODEL = 1024
DEPTH = 4
CHUNK = 64
N_MIXERS = 3
ALPHA = (2 * DEPTH) ** 0.25
LN_EPS = 1e-5
RMS_EPS = 1e-6
PLE_DIM = 256
D_FF = 4 * D_MODEL

GLA_HEADS = 4
GLA_DK = 128
GLA_DV = 256
GLA_GATE_RANK = 16
GLA_TAU = 16.0
GLA_HK = GLA_HEADS * GLA_DK
GLA_HV = GLA_HEADS * GLA_DV
CHUNK_SHIFT = CHUNK.bit_length() - 1
assert 1 << CHUNK_SHIFT == CHUNK
GLA_LEVELS = CHUNK_SHIFT

MLA_HEADS = 8
MLA_NOPE = 128
MLA_ROPE = 64
MLA_V = 128
MLA_Q_RANK = 256
MLA_KV_RANK = 256
MLA_QK_PAD = 256
ROPE_BASE = 10000.0

CONV_WIDTH = 3

LANES = 128
SUBLANES = 8
VMEM_LIMIT_BYTES = 56 * 1024 * 1024

TAIL_ROWS = 512
FF_CHUNK = 1024
GLA_PROJ_ROWS = 256
GLA_SCAN_ROWS = 512
MLA_PROJ_ROWS = 512
MLA_Q_TILE = 256
CONV_ROWS = 512


def _params(*semantics):
    return pltpu.CompilerParams(dimension_semantics=semantics, vmem_limit_bytes=VMEM_LIMIT_BYTES)


def _const_spec(shape):
    return pl.BlockSpec(shape, lambda *_: (0,) * len(shape), pipeline_mode=pl.Buffered(1))


def _dot(a, b):
    return jnp.dot(a, b, preferred_element_type=F32)


def _dot_nt(a, b):
    return jax.lax.dot_general(a, b, (((1,), (1,)), ((), ())), preferred_element_type=F32)


def _dot_tn(a, b):
    return jax.lax.dot_general(a, b, (((0,), (0,)), ((), ())), preferred_element_type=F32)


def _layer_norm(x, g, b):
    mu = jnp.mean(x, axis=-1, keepdims=True)
    xc = x - mu
    var = jnp.mean(xc * xc, axis=-1, keepdims=True)
    return xc * jax.lax.rsqrt(var + LN_EPS) * g + b


def _rms_norm(x, g):
    return x * jax.lax.rsqrt(jnp.mean(x * x, axis=-1, keepdims=True) + RMS_EPS) * g


def _sigmoid(x):
    return 1.0 / (1.0 + jnp.exp(-x))


def _tail_kernel(o_ref, x_ref, p_ref, wout_ref, w1_ref, w2_ref, wg_ref, wp_ref, ln_ref, out_ref):
    g0, g1 = ln_ref[0:1, :], ln_ref[1:2, :]
    b0, b1 = ln_ref[2:3, :], ln_ref[3:4, :]
    x = x_ref[...]
    h = _dot(o_ref[...], wout_ref[...])
    x1 = _layer_norm(ALPHA * x + h, g0, b0)
    x1b = x1.astype(BF16)
    acc = jnp.zeros_like(x1)
    for c in range(D_FF // FF_CHUNK):
        cols = slice(c * FF_CHUNK, (c + 1) * FF_CHUNK)
        hc = _dot(x1b, w1_ref[:, cols])
        hc = jnp.square(jnp.maximum(hc, 0.0)).astype(BF16)
        acc = acc + _dot(hc, w2_ref[cols, :])
    x2 = _layer_norm(ALPHA * x1 + acc, g1, b1)
    gate = _sigmoid(_dot(x2.astype(BF16), wg_ref[...]))
    proj = _dot(p_ref[...].astype(BF16), wp_ref[...])
    out_ref[...] = x2 + gate * proj


def _tail(o, x, p, wout, w1, w2, wg, wp, ln):
    n = x.shape[0]
    tm = TAIL_ROWS
    rows = lambda w: pl.BlockSpec((tm, w), lambda i: (i, 0))
    return pl.pallas_call(
        _tail_kernel,
        grid=(n // tm,),
        in_specs=[rows(D_MODEL), rows(D_MODEL), rows(PLE_DIM),
                  _const_spec(wout.shape), _const_spec(w1.shape), _const_spec(w2.shape),
                  _const_spec(wg.shape), _const_spec(wp.shape), _const_spec(ln.shape)],
        out_specs=rows(D_MODEL),
        out_shape=jax.ShapeDtypeStruct((n, D_MODEL), F32),
        compiler_params=_params("parallel"),
        name="tail",
    )(o, x, p, wout, w1, w2, wg, wp, ln)


def _split3_bf16(a):
    hi = a.astype(BF16)
    r1 = a - hi.astype(F32)
    mid = r1.astype(BF16)
    lo = (r1 - mid.astype(F32)).astype(BF16)
    return hi, mid, lo


def _gla_proj_kernel(x_ref, wqkvr_ref, wlr_ref, wgu_ref, bg_ref, tri_ref,
                     q_ref, k_ref, v_ref, r_ref, l_ref):
    xb = x_ref[...].astype(BF16)
    q_ref[...] = _dot(xb, wqkvr_ref[:, 0:GLA_HK]) * (GLA_DK ** -0.5)
    k_ref[...] = _dot(xb, wqkvr_ref[:, GLA_HK:2 * GLA_HK])
    v_ref[...] = _dot(xb, wqkvr_ref[:, 2 * GLA_HK:2 * GLA_HK + GLA_HV]).astype(BF16)
    r = _dot(xb, wqkvr_ref[:, 2 * GLA_HK + GLA_HV:])
    r_ref[...] = r * _sigmoid(r)
    g_lr = _dot(xb, wlr_ref[...])
    z = _dot(g_lr.astype(BF16), wgu_ref[...]) + bg_ref[...]
    log_a = (jnp.minimum(z, 0.0) - jnp.log1p(jnp.exp(-jnp.abs(z)))) * (1.0 / GLA_TAU)
    tri = tri_ref[...]
    hi, mid, lo = _split3_bf16(log_a)
    l_ref[...] = (_dot(tri, lo) + _dot(tri, mid)) + _dot(tri, hi)


def _gla_proj(x, wqkvr, wlr, wgu, bg, tri):
    n = x.shape[0]
    tm = GLA_PROJ_ROWS
    rows = lambda w: pl.BlockSpec((tm, w), lambda i: (i, 0))
    return pl.pallas_call(
        _gla_proj_kernel,
        grid=(n // tm,),
        in_specs=[rows(D_MODEL), _const_spec(wqkvr.shape), _const_spec(wlr.shape),
                  _const_spec(wgu.shape), _const_spec(bg.shape), _const_spec(tri.shape)],
        out_specs=[rows(GLA_HK), rows(GLA_HK), rows(GLA_HV), rows(GLA_HV), rows(GLA_HK)],
        out_shape=[jax.ShapeDtypeStruct((n, GLA_HK), F32), jax.ShapeDtypeStruct((n, GLA_HK), F32),
                   jax.ShapeDtypeStruct((n, GLA_HV), BF16), jax.ShapeDtypeStruct((n, GLA_HV), F32),
                   jax.ShapeDtypeStruct((n, GLA_HK), F32)],
        compiler_params=_params("parallel"),
        name="gla_proj",
    )(x, wqkvr, wlr, wgu, bg, tri)


def _level_ref_rows(L, level):
    C, dk = L.shape
    m = 1 << level
    if m >= SUBLANES:
        parts = [jnp.broadcast_to(L[b + m - 1:b + m, :], (2 * m, dk)) for b in range(0, C, 2 * m)]
        return parts[0] if len(parts) == 1 else jnp.concatenate(parts, axis=0)
    if m == 1:
        row = jax.lax.broadcasted_iota(jnp.int32, (C, dk), 0)
        return jnp.where((row & 1) == 1, pltpu.roll(L, shift=1, axis=0), L)
    tiles = (C // SUBLANES, SUBLANES, dk)
    L3 = L.reshape(tiles)
    pick = lambda i: jnp.broadcast_to(L3[:, i:i + 1, :], tiles)
    if m == 4:
        ref = pick(3)
    else:
        sub = jax.lax.broadcasted_iota(jnp.int32, tiles, 1)
        ref = jnp.where(sub < 4, pick(1), pick(5))
    return ref.reshape(C, dk)


def _gla_chunk(qs, ks, vs, Ls, states_t, pair_level):
    C = CHUNK
    heads = range(len(qs))
    l_ends = [L[C - 1:C, :] for L in Ls]
    os = [_dot_nt((qs[h] * jnp.exp(Ls[h])).astype(BF16), states_t[h].astype(BF16)) for h in heads]
    k_decs = [(ks[h] * jnp.exp(l_ends[h] - Ls[h])).astype(BF16) for h in heads]
    new_states_t = [states_t[h] * jnp.exp(l_ends[h]) + _dot_tn(vs[h], k_decs[h]) for h in heads]

    scores = [jnp.where(pair_level < 0, jnp.sum(qs[h] * ks[h], axis=1, keepdims=True), 0.0)
              for h in heads]
    for level in range(GLA_LEVELS):
        for h in heads:
            f = jnp.exp(-jnp.abs(Ls[h] - _level_ref_rows(Ls[h], level)))
            s_level = _dot_nt((qs[h] * f).astype(BF16), (ks[h] * f).astype(BF16))
            scores[h] = jnp.where(pair_level == level, s_level, scores[h])

    os = [os[h] + _dot(scores[h].astype(BF16), vs[h]) for h in heads]
    return os, new_states_t


def _gla_scan_kernel(q_ref, k_ref, v_ref, r_ref, l_ref, g_ref, lvl_ref, o_ref, state_ref):
    @pl.when(pl.program_id(1) == 0)
    def _():
        state_ref[...] = jnp.zeros_like(state_ref)

    g = g_ref[...]
    pair_level = lvl_ref[...]

    for c in range(GLA_SCAN_ROWS // CHUNK):
        rows = slice(c * CHUNK, (c + 1) * CHUNK)
        kcs = [slice(h * GLA_DK, (h + 1) * GLA_DK) for h in range(GLA_HEADS)]
        vcs = [slice(h * GLA_DV, (h + 1) * GLA_DV) for h in range(GLA_HEADS)]
        os, new_states = _gla_chunk([q_ref[rows, kc] for kc in kcs], [k_ref[rows, kc] for kc in kcs],
                                    [v_ref[rows, vc] for vc in vcs], [l_ref[rows, kc] for kc in kcs],
                                    [state_ref[h] for h in range(GLA_HEADS)], pair_level)
        for h in range(GLA_HEADS):
            state_ref[h] = new_states[h]
            o_ref[rows, vcs[h]] = (_rms_norm(os[h], g) * r_ref[rows, vcs[h]]).astype(BF16)


def _gla_scan(q, k, v, r, L, g, batch, seq):
    n = q.shape[0]
    t = GLA_SCAN_ROWS
    per_b = seq // t
    rows = lambda w: pl.BlockSpec((t, w), lambda b, j: (b * per_b + j, 0))
    idx = np.arange(CHUNK)
    differ = idx[:, None] ^ idx[None, :]
    pair_level = jnp.asarray(np.floor(np.log2(np.maximum(differ, 1))).astype(np.int32) - (differ == 0))
    return pl.pallas_call(
        _gla_scan_kernel,
        grid=(batch, per_b),
        in_specs=[rows(GLA_HK), rows(GLA_HK), rows(GLA_HV), rows(GLA_HV), rows(GLA_HK),
                  _const_spec(g.shape), _const_spec(pair_level.shape)],
        out_specs=rows(GLA_HV),
        out_shape=jax.ShapeDtypeStruct((n, GLA_HV), BF16),
        scratch_shapes=[pltpu.VMEM((GLA_HEADS, GLA_DV, GLA_DK), F32)],
        compiler_params=_params("parallel", "arbitrary"),
        name="gla_scan",
    )(q, k, v, r, L, g, pair_level)


def _gla_front(x, w_in, w_gate_up, b_gate, norm_g, batch, seq):
    wqkvr = w_in[:, :2 * GLA_HK + GLA_HV + D_MODEL].astype(BF16)
    wlr = jnp.pad(w_in[:, 2 * GLA_HK + GLA_HV + D_MODEL:], ((0, 0), (0, LANES - GLA_GATE_RANK))).astype(BF16)
    wgu = jnp.pad(w_gate_up, ((0, LANES - GLA_GATE_RANK), (0, 0))).astype(BF16)
    idx = np.arange(GLA_PROJ_ROWS)
    tri = ((idx[:, None] // CHUNK == idx[None, :] // CHUNK) & (idx[None, :] <= idx[:, None]))
    tri = jnp.asarray(tri, BF16)
    q, k, v, r, L = _gla_proj(x, wqkvr, wlr, wgu, b_gate[None, :], tri)
    return _gla_scan(q, k, v, r, L, norm_g[None, :], batch, seq)


def _mla_proj_kernel(x_ref, pos_ref, freq_ref, sign_ref, win_ref, qn_ref, kvn_ref,
                     wq_ref, wkv_ref, q_ref, k_ref, v_ref):
    H, R = MLA_HEADS, MLA_Q_RANK
    xb = x_ref[...].astype(BF16)
    c = _dot(xb, win_ref[...])
    cq = _rms_norm(c[:, 0:R], qn_ref[...]).astype(BF16)
    ckv = _rms_norm(c[:, R:R + MLA_KV_RANK], kvn_ref[...]).astype(BF16)
    kr = c[:, 2 * R:2 * R + LANES]
    kr_sw = c[:, 2 * R + LANES:2 * R + 2 * LANES]
    ang = pos_ref[...] * freq_ref[...]
    cos = jnp.cos(ang)
    sin = jnp.sin(ang) * sign_ref[...]
    k_rope = (kr * cos + kr_sw * sin).astype(BF16)
    qq = _dot(cq, wq_ref[...])
    kv = _dot(ckv, wkv_ref[...])
    scale = (MLA_NOPE + MLA_ROPE) ** -0.5
    for h in range(H):
        lo = h * LANES
        q_nope = qq[:, lo:lo + LANES]
        q_r = qq[:, H * LANES + lo:H * LANES + lo + LANES]
        q_sw = qq[:, 2 * H * LANES + lo:2 * H * LANES + lo + LANES]
        base = h * MLA_QK_PAD
        q_ref[:, base:base + LANES] = (q_nope * scale).astype(BF16)
        q_ref[:, base + LANES:base + 2 * LANES] = ((q_r * cos + q_sw * sin) * scale).astype(BF16)
        k_ref[:, base:base + LANES] = kv[:, lo:lo + LANES].astype(BF16)
        k_ref[:, base + LANES:base + 2 * LANES] = k_rope
    v_ref[...] = kv[:, H * LANES:].astype(BF16)


def _mla_proj(x, pos, freq, sign, win, qn, kvn, wq, wkv):
    n = x.shape[0]
    tm = MLA_PROJ_ROWS
    rows = lambda w: pl.BlockSpec((tm, w), lambda i: (i, 0))
    qk_w = MLA_HEADS * MLA_QK_PAD
    return pl.pallas_call(
        _mla_proj_kernel,
        grid=(n // tm,),
        in_specs=[rows(D_MODEL), rows(1), _const_spec(freq.shape), _const_spec(sign.shape),
                  _const_spec(win.shape), _const_spec(qn.shape), _const_spec(kvn.shape),
                  _const_spec(wq.shape), _const_spec(wkv.shape)],
        out_specs=[rows(qk_w), rows(qk_w), rows(MLA_HEADS * MLA_V)],
        out_shape=[jax.ShapeDtypeStruct((n, qk_w), BF16), jax.ShapeDtypeStruct((n, qk_w), BF16),
                   jax.ShapeDtypeStruct((n, MLA_HEADS * MLA_V), BF16)],
        compiler_params=_params("parallel"),
        name="mla_proj",
    )(x, pos, freq, sign, win, qn, kvn, wq, wkv)


def _mla_attn_kernel(q_ref, k_ref, v_ref, o_ref):
    seq = q_ref.shape[0]
    tq = MLA_Q_TILE
    r = jax.lax.broadcasted_iota(jnp.int32, (tq, tq), 0)
    c = jax.lax.broadcasted_iota(jnp.int32, (tq, tq), 1)
    visible = jnp.right_shift(c, CHUNK_SHIFT) <= jnp.right_shift(r, CHUNK_SHIFT)
    for qi in range(seq // tq):
        q = q_ref[qi * tq:(qi + 1) * tq, :]
        s_diag = jnp.where(visible, _dot_nt(q, k_ref[qi * tq:(qi + 1) * tq, :]), -jnp.inf)
        m = jnp.max(s_diag, axis=1, keepdims=True)
        if qi:
            s_past = _dot_nt(q, k_ref[0:qi * tq, :])
            m = jnp.maximum(m, jnp.max(s_past, axis=1, keepdims=True))
        p = jnp.exp(s_diag - m)
        den = jnp.sum(p, axis=1, keepdims=True)
        acc = _dot(p.astype(BF16), v_ref[qi * tq:(qi + 1) * tq, :])
        if qi:
            p = jnp.exp(s_past - m)
            den = den + jnp.sum(p, axis=1, keepdims=True)
            acc = acc + _dot(p.astype(BF16), v_ref[0:qi * tq, :])
        o_ref[qi * tq:(qi + 1) * tq, :] = (acc / den).astype(BF16)


def _mla_attn(q, k, v, batch, seq):
    n = q.shape[0]
    blk = lambda w: pl.BlockSpec((seq, w), lambda b, h: (b, h))
    return pl.pallas_call(
        _mla_attn_kernel,
        grid=(batch, MLA_HEADS),
        in_specs=[blk(MLA_QK_PAD), blk(MLA_QK_PAD), blk(MLA_V)],
        out_specs=blk(MLA_V),
        out_shape=jax.ShapeDtypeStruct((n, MLA_HEADS * MLA_V), BF16),
        compiler_params=_params("parallel", "parallel"),
        name="mla_attn",
    )(q, k, v)


def _mla_front(x, pos, w_in, q_norm, kv_norm, w_uq, w_ukv, batch, seq):
    H, half = MLA_HEADS, MLA_ROPE // 2
    pad = LANES - MLA_ROPE

    def rope_cols(w, swap):
        a, b = w[..., :half], w[..., half:]
        if swap:
            a, b = b, a
        return jnp.concatenate([a, b, jnp.zeros(w.shape[:-1] + (pad,), w.dtype)], axis=-1)

    kr_w = w_in[:, MLA_Q_RANK + MLA_KV_RANK:]
    win = jnp.concatenate([w_in[:, :MLA_Q_RANK + MLA_KV_RANK], rope_cols(kr_w, False),
                           rope_cols(kr_w, True)], axis=1).astype(BF16)
    wq3 = w_uq.reshape(MLA_Q_RANK, H, MLA_NOPE + MLA_ROPE)
    wq = jnp.concatenate([wq3[:, :, :MLA_NOPE].reshape(MLA_Q_RANK, H * LANES),
                          rope_cols(wq3[:, :, MLA_NOPE:], False).reshape(MLA_Q_RANK, H * LANES),
                          rope_cols(wq3[:, :, MLA_NOPE:], True).reshape(MLA_Q_RANK, H * LANES)],
                         axis=1).astype(BF16)
    wkv3 = w_ukv.reshape(MLA_KV_RANK, H, MLA_NOPE + MLA_V)
    wkv = jnp.concatenate([wkv3[:, :, :MLA_NOPE].reshape(MLA_KV_RANK, H * MLA_NOPE),
                           wkv3[:, :, MLA_NOPE:].reshape(MLA_KV_RANK, H * MLA_V)], axis=1).astype(BF16)
    inv_freq = ROPE_BASE ** (-jnp.arange(0, MLA_ROPE // 2, dtype=F32) * (2.0 / MLA_ROPE))
    zeros = jnp.zeros((pad,), F32)
    freq = jnp.concatenate([inv_freq, inv_freq, zeros])[None, :]
    sign = jnp.concatenate([-jnp.ones((half,), F32), jnp.ones((half,), F32), zeros])[None, :]
    q, k, v = _mla_proj(x, pos, freq, sign, win, q_norm[None, :], kv_norm[None, :], wq, wkv)
    return _mla_attn(q, k, v, batch, seq)


def _conv_kernel(x_ref, win_ref, cw_ref, o_ref, cu_ref):
    ts = x_ref.shape[0]
    halo = SUBLANES

    @pl.when(pl.program_id(1) == 0)
    def _():
        cu_ref[0:halo, :] = jnp.zeros((halo, D_MODEL), F32)

    xb = x_ref[...].astype(BF16)
    b = _dot(xb, win_ref[:, 0:D_MODEL])
    cu = _dot(xb, win_ref[:, D_MODEL:2 * D_MODEL]) * _dot(xb, win_ref[:, 2 * D_MODEL:])
    cu_ref[halo:halo + ts, :] = cu
    z = (cw_ref[0:1, :] * cu_ref[halo - 2:halo - 2 + ts, :]
         + cw_ref[1:2, :] * cu_ref[halo - 1:halo - 1 + ts, :]
         + cw_ref[2:3, :] * cu)
    o_ref[...] = (b * z).astype(BF16)
    cu_ref[0:halo, :] = cu_ref[ts:ts + halo, :]


def _conv_front(x, w_in, conv_w, batch, seq):
    n = x.shape[0]
    ts = CONV_ROWS
    per_b = seq // ts
    rows = lambda w: pl.BlockSpec((ts, w), lambda b, j: (b * per_b + j, 0))
    win = w_in.astype(BF16)
    return pl.pallas_call(
        _conv_kernel,
        grid=(batch, per_b),
        in_specs=[rows(D_MODEL), _const_spec(win.shape), _const_spec(conv_w.shape)],
        out_specs=rows(D_MODEL),
        out_shape=jax.ShapeDtypeStruct((n, D_MODEL), BF16),
        scratch_shapes=[pltpu.VMEM((SUBLANES + ts, D_MODEL), F32)],
        compiler_params=_params("parallel", "arbitrary"),
        name="conv_front",
    )(x, win, conv_w)


def kernel(x, p, positions, gla_w_in, gla_w_gate_up, gla_b_gate, gla_norm_g, gla_w_out,
           mla_w_in, mla_q_norm, mla_kv_norm, mla_w_uq, mla_w_ukv, mla_w_out,
           conv_w_in, conv_w, conv_w_out, ln_g, ln_b, mlp_w1, mlp_w2, ple_w_gate, ple_w_proj):
    batch, seq, d = x.shape
    n = batch * seq
    assert d == D_MODEL and seq % GLA_SCAN_ROWS == 0 and seq % CONV_ROWS == 0 and seq % MLA_Q_TILE == 0
    xf = x.reshape(n, d)
    pos = positions.astype(F32).reshape(n, 1)
    for i in range(DEPTH):
        j, kind = i // N_MIXERS, i % N_MIXERS
        if kind == 0:
            o = _gla_front(xf, gla_w_in[j], gla_w_gate_up[j], gla_b_gate[j], gla_norm_g[j], batch, seq)
            w_out = gla_w_out[j]
        elif kind == 1:
            o = _mla_front(xf, pos, mla_w_in[j], mla_q_norm[j], mla_kv_norm[j], mla_w_uq[j],
                           mla_w_ukv[j], batch, seq)
            w_out = mla_w_out[j]
        else:
            o = _conv_front(xf, conv_w_in[j], conv_w[j], batch, seq)
            w_out = conv_w_out[j]
        ln = jnp.concatenate([ln_g[i], ln_b[i]], axis=0)
        xf = _tail(o, xf, p[i].reshape(n, PLE_DIM), w_out.astype(BF16), mlp_w1[i].astype(BF16),
                   mlp_w2[i].astype(BF16), ple_w_gate[i].astype(BF16), ple_w_proj[i].astype(BF16), ln)
    return xf.reshape(batch, seq, d)
```

```python
import functools

import numpy as np
import jax
import jax.numpy as jnp
from jax.experimental import pallas as pl
from jax.experimental.pallas import tpu as pltpu

F32 = jnp.float32
BF16 = jnp.bfloat16

D_MODEL = 1024
DEPTH = 4
CHUNK = 64
N_MIXERS = 3
ALPHA = (2 * DEPTH) ** 0.25
LN_EPS = 1e-5
RMS_EPS = 1e-6
PLE_DIM = 256
D_FF = 4 * D_MODEL

GLA_HEADS = 4
GLA_DK = 128
GLA_DV = 256
GLA_GATE_RANK = 16
GLA_TAU = 16.0
GLA_HK = GLA_HEADS * GLA_DK
GLA_HV = GLA_HEADS * GLA_DV
CHUNK_SHIFT = CHUNK.bit_length() - 1
assert 1 << CHUNK_SHIFT == CHUNK
GLA_LEVELS = CHUNK_SHIFT

MLA_HEADS = 8
MLA_NOPE = 128
MLA_ROPE = 64
MLA_V = 128
MLA_Q_RANK = 256
MLA_KV_RANK = 256
MLA_QK_PAD = 256
ROPE_BASE = 10000.0

CONV_WIDTH = 3

LANES = 128
SUBLANES = 8
MXU_WIDTH = 256
VMEM_LIMIT_BYTES = 56 * 1024 * 1024

TAIL_ROWS = 512
FF_CHUNK = 1024
GLA_ROWS = 512
TRI_ROWS = 256
MLA_PROJ_ROWS = 512
ROPE_TABLE_ROWS = 512
MLA_Q_TILE = 256
CONV_ROWS = 512


def _params(*semantics):
    return pltpu.CompilerParams(dimension_semantics=semantics, vmem_limit_bytes=VMEM_LIMIT_BYTES)


def _const_spec(shape):
    return pl.BlockSpec(shape, lambda *_: (0,) * len(shape), pipeline_mode=pl.Buffered(1))


def _dot(a, b):
    return jnp.dot(a, b, preferred_element_type=F32)


def _dot_nt(a, b):
    return jax.lax.dot_general(a, b, (((1,), (1,)), ((), ())), preferred_element_type=F32)


def _dot_tn(a, b):
    return jax.lax.dot_general(a, b, (((0,), (0,)), ((), ())), preferred_element_type=F32)


def _layer_norm(x, g, b):
    mu = jnp.mean(x, axis=-1, keepdims=True)
    xc = x - mu
    var = jnp.mean(xc * xc, axis=-1, keepdims=True)
    return xc * jax.lax.rsqrt(var + LN_EPS) * g + b


def _rms_norm(x, g):
    return x * jax.lax.rsqrt(jnp.mean(x * x, axis=-1, keepdims=True) + RMS_EPS) * g


def _sigmoid(x):
    return 1.0 / (1.0 + jnp.exp(-x))


def _tail_kernel(o_ref, x_ref, p_ref, wout_ref, w1_ref, w2_ref, wg_ref, wp_ref, ln_ref, out_ref):
    g0, g1 = ln_ref[0:1, :], ln_ref[1:2, :]
    b0, b1 = ln_ref[2:3, :], ln_ref[3:4, :]
    x = x_ref[...]
    h = _dot(o_ref[...], wout_ref[...])
    x1 = _layer_norm(ALPHA * x + h, g0, b0)
    x1b = x1.astype(BF16)
    acc = jnp.zeros_like(x1)
    for c in range(D_FF // FF_CHUNK):
        cols = slice(c * FF_CHUNK, (c + 1) * FF_CHUNK)
        hc = _dot(x1b, w1_ref[:, cols])
        hc = jnp.square(jnp.maximum(hc, 0.0)).astype(BF16)
        acc = acc + _dot(hc, w2_ref[cols, :])
    x2 = _layer_norm(ALPHA * x1 + acc, g1, b1)
    gate = _sigmoid(_dot(x2.astype(BF16), wg_ref[...]))
    proj = _dot(p_ref[...].astype(BF16), wp_ref[...])
    out_ref[...] = x2 + gate * proj


def _tail(o, x, p_all, layer, wout, w1, w2, wg, wp, ln):
    n = x.shape[0]
    tm = TAIL_ROWS
    rows = lambda w: pl.BlockSpec((tm, w), lambda i: (i, 0))
    p_rows = pl.BlockSpec((tm, PLE_DIM), lambda i: (layer * (n // tm) + i, 0))
    return pl.pallas_call(
        _tail_kernel,
        grid=(n // tm,),
        in_specs=[rows(D_MODEL), rows(D_MODEL), p_rows,
                  _const_spec(wout.shape), _const_spec(w1.shape), _const_spec(w2.shape),
                  _const_spec(wg.shape), _const_spec(wp.shape), _const_spec(ln.shape)],
        out_specs=rows(D_MODEL),
        out_shape=jax.ShapeDtypeStruct((n, D_MODEL), F32),
        compiler_params=_params("parallel"),
        name="tail",
    )(o, x, p_all, wout, w1, w2, wg, wp, ln)


def _split3_bf16(a):
    hi = a.astype(BF16)
    r1 = a - hi.astype(F32)
    mid = r1.astype(BF16)
    lo = (r1 - mid.astype(F32)).astype(BF16)
    return hi, mid, lo


def _gla_project_items(x_ref, xb_ref, wqkvr_ref, wlr_ref, wgu_ref, bg_ref, tri_ref,
                       q_ref, k_ref, v_ref, r_ref, l_ref):
    w = MXU_WIDTH

    def cast_x():
        xb_ref[...] = x_ref[...].astype(BF16)

    def column_tile(dst_ref, dst_col, w_col, finish):
        def item():
            y = _dot(xb_ref[...], wqkvr_ref[:, w_col:w_col + w])
            dst_ref[:, dst_col:dst_col + w] = finish(y)
        return item

    def log_decay():
        g_lr = _dot(xb_ref[...], wlr_ref[...])
        z = _dot(g_lr.astype(BF16), wgu_ref[...]) + bg_ref[...]
        log_a = (jnp.minimum(z, 0.0) - jnp.log1p(jnp.exp(-jnp.abs(z)))) * (1.0 / GLA_TAU)
        tri = tri_ref[...]
        for r0 in range(0, GLA_ROWS, TRI_ROWS):
            hi, mid, lo = _split3_bf16(log_a[r0:r0 + TRI_ROWS, :])
            l_ref[r0:r0 + TRI_ROWS, :] = (_dot(tri, lo) + _dot(tri, mid)) + _dot(tri, hi)

    items = [cast_x, log_decay]
    groups = [(q_ref, GLA_HK, lambda y: y * (GLA_DK ** -0.5)), (k_ref, GLA_HK, lambda y: y),
              (v_ref, GLA_HV, lambda y: y.astype(BF16)), (r_ref, GLA_HV, lambda y: y * _sigmoid(y))]
    w_col = 0
    for dst_ref, width, finish in groups:
        for dst_col in range(0, width, w):
            items.append(column_tile(dst_ref, dst_col, w_col, finish))
            w_col += w
    return items


def _level_ref_rows(L, level):
    C, dk = L.shape
    m = 1 << level
    if m >= SUBLANES:
        parts = [jnp.broadcast_to(L[b + m - 1:b + m, :], (2 * m, dk)) for b in range(0, C, 2 * m)]
        return parts[0] if len(parts) == 1 else jnp.concatenate(parts, axis=0)
    if m == 1:
        row = jax.lax.broadcasted_iota(jnp.int32, (C, dk), 0)
        return jnp.where((row & 1) == 1, pltpu.roll(L, shift=1, axis=0), L)
    tiles = (C // SUBLANES, SUBLANES, dk)
    L3 = L.reshape(tiles)
    pick = lambda i: jnp.broadcast_to(L3[:, i:i + 1, :], tiles)
    if m == 4:
        ref = pick(3)
    else:
        sub = jax.lax.broadcasted_iota(jnp.int32, tiles, 1)
        ref = jnp.where(sub < 4, pick(1), pick(5))
    return ref.reshape(C, dk)


def _gla_chunk(qs, ks, vs, Ls, states_t, pair_level, tick):
    C = CHUNK
    heads = range(len(qs))
    l_ends = [L[C - 1:C, :] for L in Ls]
    os = [_dot_nt((qs[h] * jnp.exp(Ls[h])).astype(BF16), states_t[h].astype(BF16)) for h in heads]
    k_decs = [(ks[h] * jnp.exp(l_ends[h] - Ls[h])).astype(BF16) for h in heads]
    new_states_t = [states_t[h] * jnp.exp(l_ends[h]) + _dot_tn(vs[h], k_decs[h]) for h in heads]
    tick()

    scores = [jnp.where(pair_level < 0, jnp.sum(qs[h] * ks[h], axis=1, keepdims=True), 0.0)
              for h in heads]
    for level in range(GLA_LEVELS):
        for h in heads:
            f = jnp.exp(-jnp.abs(Ls[h] - _level_ref_rows(Ls[h], level)))
            s_level = _dot_nt((qs[h] * f).astype(BF16), (ks[h] * f).astype(BF16))
            scores[h] = jnp.where(pair_level == level, s_level, scores[h])
        tick()

    os = [os[h] + _dot(scores[h].astype(BF16), vs[h]) for h in heads]
    return os, new_states_t


def _gla_scan_tile(q_ref, k_ref, v_ref, r_ref, l_ref, g_ref, lvl_ref, o_ref, state_ref, tick):
    g = g_ref[...]
    pair_level = lvl_ref[...]

    for c in range(GLA_ROWS // CHUNK):
        rows = slice(c * CHUNK, (c + 1) * CHUNK)
        kcs = [slice(h * GLA_DK, (h + 1) * GLA_DK) for h in range(GLA_HEADS)]
        vcs = [slice(h * GLA_DV, (h + 1) * GLA_DV) for h in range(GLA_HEADS)]
        os, new_states = _gla_chunk([q_ref[rows, kc] for kc in kcs], [k_ref[rows, kc] for kc in kcs],
                                    [v_ref[rows, vc] for vc in vcs], [l_ref[rows, kc] for kc in kcs],
                                    [state_ref[h] for h in range(GLA_HEADS)], pair_level, tick)
        for h in range(GLA_HEADS):
            state_ref[h] = new_states[h]
            o_ref[rows, vcs[h]] = (_rms_norm(os[h], g) * r_ref[rows, vcs[h]]).astype(BF16)
        tick()


def _gla_front_kernel(tiles_per_seq, x_ref, wqkvr_ref, wlr_ref, wgu_ref, bg_ref, tri_ref, g_ref, lvl_ref,
                      o_ref, state_ref, xb_ref, *slots):
    step = pl.program_id(0)
    slot_a, slot_b = slots[:5], slots[5:]

    @pl.when(step == 0)
    def _():
        for ref in slot_b:
            ref[...] = jnp.zeros_like(ref)

    @pl.when((step == 0) | (jax.lax.rem(step - 1, tiles_per_seq) == 0))
    def _():
        state_ref[...] = jnp.zeros_like(state_ref)

    def body(dst, src):
        items = _gla_project_items(x_ref, xb_ref, wqkvr_ref, wlr_ref, wgu_ref, bg_ref, tri_ref, *dst)
        n_items, n_ticks = len(items), (GLA_ROWS // CHUNK) * (GLA_LEVELS + 2)
        done = [0, 0]

        def tick():
            done[0] += 1
            while done[1] < n_items and done[1] * (n_ticks - 2) < done[0] * n_items:
                items[done[1]]()
                done[1] += 1

        items[0]()
        done[1] = 1
        _gla_scan_tile(*src, g_ref, lvl_ref, o_ref, state_ref, tick)
        assert done == [n_ticks, n_items]

    parity = jax.lax.rem(step, 2)
    pl.when(parity == 0)(lambda: body(slot_a, slot_b))
    pl.when(parity == 1)(lambda: body(slot_b, slot_a))


def _gla_front(x, w_in, w_gate_up, b_gate, norm_g, batch, seq):
    n = x.shape[0]
    t = GLA_ROWS
    n_tiles = n // t
    wqkvr = w_in[:, :2 * GLA_HK + GLA_HV + D_MODEL].astype(BF16)
    wlr = jnp.pad(w_in[:, 2 * GLA_HK + GLA_HV + D_MODEL:], ((0, 0), (0, LANES - GLA_GATE_RANK))).astype(BF16)
    wgu = jnp.pad(w_gate_up, ((0, LANES - GLA_GATE_RANK), (0, 0))).astype(BF16)
    idx = np.arange(TRI_ROWS)
    tri = ((idx[:, None] // CHUNK == idx[None, :] // CHUNK) & (idx[None, :] <= idx[:, None]))
    tri = jnp.asarray(tri, BF16)
    idx = np.arange(CHUNK)
    differ = idx[:, None] ^ idx[None, :]
    pair_level = jnp.asarray(np.floor(np.log2(np.maximum(differ, 1))).astype(np.int32) - (differ == 0))
    bg, g = b_gate[None, :], norm_g[None, :]
    slot = [pltpu.VMEM((t, GLA_HK), F32), pltpu.VMEM((t, GLA_HK), F32), pltpu.VMEM((t, GLA_HV), BF16),
            pltpu.VMEM((t, GLA_HV), F32), pltpu.VMEM((t, GLA_HK), F32)]
    return pl.pallas_call(
        functools.partial(_gla_front_kernel, seq // t),
        grid=(n_tiles + 1,),
        in_specs=[pl.BlockSpec((t, D_MODEL), lambda s: (jnp.minimum(s, n_tiles - 1), 0)),
                  _const_spec(wqkvr.shape), _const_spec(wlr.shape), _const_spec(wgu.shape),
                  _const_spec(bg.shape), _const_spec(tri.shape), _const_spec(g.shape),
                  _const_spec(pair_level.shape)],
        out_specs=pl.BlockSpec((t, GLA_HV), lambda s: (jnp.maximum(s - 1, 0), 0)),
        out_shape=jax.ShapeDtypeStruct((n, GLA_HV), BF16),
        scratch_shapes=[pltpu.VMEM((GLA_HEADS, GLA_DV, GLA_DK), F32), pltpu.VMEM((t, D_MODEL), BF16)]
                       + slot + slot,
        compiler_params=_params("arbitrary"),
        name="gla_front",
    )(x, wqkvr, wlr, wgu, bg, tri, g, pair_level)


def _rope_table_kernel(pos_ref, freq_ref, cos_ref, sin_ref):
    ang = pos_ref[...] * freq_ref[...]
    cos_ref[...] = jnp.cos(ang)
    sin_ref[...] = jnp.sin(ang)


def _rope_tables(pos):
    n, half = pos.shape[0], MLA_ROPE // 2
    per_row = LANES // half
    inv_freq = ROPE_BASE ** (-jnp.arange(0, half, dtype=F32) * (2.0 / MLA_ROPE))
    pos_d = jnp.repeat(pos.reshape(n // per_row, per_row), half, axis=1)
    freq_d = jnp.tile(inv_freq, per_row)[None, :]
    rows_d = n // per_row
    tm = min(rows_d, ROPE_TABLE_ROWS)
    blk = pl.BlockSpec((tm, LANES), lambda i: (i, 0))
    cos_d, sin_d = pl.pallas_call(
        _rope_table_kernel,
        grid=(rows_d // tm,),
        in_specs=[blk, _const_spec(freq_d.shape)],
        out_specs=[blk, blk],
        out_shape=[jax.ShapeDtypeStruct((rows_d, LANES), F32)] * 2,
        compiler_params=_params("parallel"),
        name="rope_tables",
    )(pos_d, freq_d)
    return cos_d.reshape(n, half), sin_d.reshape(n, half)


def _mla_proj_kernel(x_ref, cos_ref, sin_ref, win_ref, qn_ref, kvn_ref,
                     wq_ref, wk_ref, wvt_ref, q_ref, k_ref, vt_ref):
    H, R = MLA_HEADS, MLA_Q_RANK
    xb = x_ref[...].astype(BF16)
    c = _dot(xb, win_ref[...])
    cq = _rms_norm(c[:, 0:R], qn_ref[...]).astype(BF16)
    ckv = _rms_norm(c[:, R:R + MLA_KV_RANK], kvn_ref[...]).astype(BF16)
    kr = c[:, 2 * R:2 * R + LANES]
    kr_sw = c[:, 2 * R + LANES:2 * R + 2 * LANES]
    c32, s32 = cos_ref[...], sin_ref[...]
    pad = jnp.zeros((c32.shape[0], LANES - MLA_ROPE), F32)
    cos = jnp.concatenate([c32, c32, pad], axis=1)
    sin = jnp.concatenate([-s32, s32, pad], axis=1)
    k_rope = (kr * cos + kr_sw * sin).astype(BF16)
    qq = _dot(cq, wq_ref[...])
    k_nope = _dot(ckv, wk_ref[...])
    vt_ref[...] = _dot_nt(wvt_ref[...], ckv).astype(BF16)
    scale = (MLA_NOPE + MLA_ROPE) ** -0.5
    for h in range(H):
        lo = h * LANES
        q_nope = qq[:, lo:lo + LANES]
        q_r = qq[:, H * LANES + lo:H * LANES + lo + LANES]
        q_sw = qq[:, 2 * H * LANES + lo:2 * H * LANES + lo + LANES]
        base = h * MLA_QK_PAD
        q_ref[:, base:base + LANES] = (q_nope * scale).astype(BF16)
        q_ref[:, base + LANES:base + 2 * LANES] = ((q_r * cos + q_sw * sin) * scale).astype(BF16)
        k_ref[:, base:base + LANES] = k_nope[:, lo:lo + LANES].astype(BF16)
        k_ref[:, base + LANES:base + 2 * LANES] = k_rope


def _mla_proj(x, cos, sin, win, qn, kvn, wq, wk, wvt, seq):
    n = x.shape[0]
    tm = MLA_PROJ_ROWS
    per_b = seq // tm
    rows = lambda w: pl.BlockSpec((tm, w), lambda i: (i, 0))
    qk_w = MLA_HEADS * MLA_QK_PAD
    hv = MLA_HEADS * MLA_V
    return pl.pallas_call(
        _mla_proj_kernel,
        grid=(n // tm,),
        in_specs=[rows(D_MODEL), rows(MLA_ROPE // 2), rows(MLA_ROPE // 2), _const_spec(win.shape), _const_spec(qn.shape), _const_spec(kvn.shape),
                  _const_spec(wq.shape), _const_spec(wk.shape), _const_spec(wvt.shape)],
        out_specs=[rows(qk_w), rows(qk_w),
                   pl.BlockSpec((hv, tm), lambda i: (i // per_b, i % per_b))],
        out_shape=[jax.ShapeDtypeStruct((n, qk_w), BF16), jax.ShapeDtypeStruct((n, qk_w), BF16),
                   jax.ShapeDtypeStruct((n // seq * hv, seq), BF16)],
        compiler_params=_params("parallel"),
        name="mla_proj",
    )(x, cos, sin, win, qn, kvn, wq, wk, wvt)


def _mla_attn_kernel(q_ref, k_ref, vt_ref, o_ref):
    seq = q_ref.shape[0]
    tq = MLA_Q_TILE
    key = jax.lax.broadcasted_iota(jnp.int32, (tq, tq), 0)
    qry = jax.lax.broadcasted_iota(jnp.int32, (tq, tq), 1)
    visible = jnp.right_shift(key, CHUNK_SHIFT) <= jnp.right_shift(qry, CHUNK_SHIFT)

    def scores(qi):
        lo, hi = qi * tq, (qi + 1) * tq
        q = q_ref[lo:hi, :]
        s_diag = jnp.where(visible, _dot_nt(k_ref[lo:hi, :], q), -jnp.inf)
        s_past = _dot_nt(k_ref[0:lo, :], q) if qi else None
        return s_diag, s_past

    def finish(qi, s_diag, s_past):
        lo, hi = qi * tq, (qi + 1) * tq
        m = jnp.max(s_diag, axis=0, keepdims=True)
        if qi:
            m = jnp.maximum(m, jnp.max(s_past, axis=0, keepdims=True))
        p = jnp.exp(s_diag - m)
        den = jnp.sum(p, axis=0, keepdims=True)
        acc = _dot(vt_ref[:, lo:hi], p.astype(BF16))
        if qi:
            p = jnp.exp(s_past - m)
            den = den + jnp.sum(p, axis=0, keepdims=True)
            acc = acc + _dot(vt_ref[:, 0:lo], p.astype(BF16))
        o_ref[lo:hi, :] = jnp.transpose(acc / den).astype(BF16)

    order = list(range(seq // tq))[::-1]
    ahead = scores(order[0])
    for i, qi in enumerate(order):
        current = ahead
        if i + 1 < len(order):
            ahead = scores(order[i + 1])
        finish(qi, *current)


def _mla_attn(q, k, vt, batch, seq):
    n = q.shape[0]
    blk = lambda w: pl.BlockSpec((seq, w), lambda b, h: (b, h))
    return pl.pallas_call(
        _mla_attn_kernel,
        grid=(batch, MLA_HEADS),
        in_specs=[blk(MLA_QK_PAD), blk(MLA_QK_PAD),
                  pl.BlockSpec((MLA_V, seq), lambda b, h: (b * MLA_HEADS + h, 0))],
        out_specs=blk(MLA_V),
        out_shape=jax.ShapeDtypeStruct((n, MLA_HEADS * MLA_V), BF16),
        compiler_params=_params("parallel", "parallel"),
        name="mla_attn",
    )(q, k, vt)


def _mla_front(x, pos, w_in, q_norm, kv_norm, w_uq, w_ukv, batch, seq):
    H, half = MLA_HEADS, MLA_ROPE // 2
    pad = LANES - MLA_ROPE

    def rope_cols(w, swap):
        a, b = w[..., :half], w[..., half:]
        if swap:
            a, b = b, a
        return jnp.concatenate([a, b, jnp.zeros(w.shape[:-1] + (pad,), w.dtype)], axis=-1)

    kr_w = w_in[:, MLA_Q_RANK + MLA_KV_RANK:]
    win = jnp.concatenate([w_in[:, :MLA_Q_RANK + MLA_KV_RANK], rope_cols(kr_w, False),
                           rope_cols(kr_w, True)], axis=1).astype(BF16)
    wq3 = w_uq.reshape(MLA_Q_RANK, H, MLA_NOPE + MLA_ROPE)
    wq = jnp.concatenate([wq3[:, :, :MLA_NOPE].reshape(MLA_Q_RANK, H * LANES),
                          rope_cols(wq3[:, :, MLA_NOPE:], False).reshape(MLA_Q_RANK, H * LANES),
                          rope_cols(wq3[:, :, MLA_NOPE:], True).reshape(MLA_Q_RANK, H * LANES)],
                         axis=1).astype(BF16)
    wkv3 = w_ukv.reshape(MLA_KV_RANK, H, MLA_NOPE + MLA_V)
    wk = wkv3[:, :, :MLA_NOPE].reshape(MLA_KV_RANK, H * MLA_NOPE).astype(BF16)
    wvt = wkv3[:, :, MLA_NOPE:].reshape(MLA_KV_RANK, H * MLA_V).T.astype(BF16)
    cos, sin = _rope_tables(pos)
    q, k, v = _mla_proj(x, cos, sin, win, q_norm[None, :], kv_norm[None, :], wq, wk, wvt, seq)
    return _mla_attn(q, k, v, batch, seq)


def _conv_kernel(x_ref, win_ref, cw_ref, o_ref, cu_ref):
    ts = x_ref.shape[0]
    halo = SUBLANES

    @pl.when(pl.program_id(1) == 0)
    def _():
        cu_ref[0:halo, :] = jnp.zeros((halo, D_MODEL), F32)

    xb = x_ref[...].astype(BF16)
    b = _dot(xb, win_ref[:, 0:D_MODEL])
    cu = _dot(xb, win_ref[:, D_MODEL:2 * D_MODEL]) * _dot(xb, win_ref[:, 2 * D_MODEL:])
    cu_ref[halo:halo + ts, :] = cu
    z = (cw_ref[0:1, :] * cu_ref[halo - 2:halo - 2 + ts, :]
         + cw_ref[1:2, :] * cu_ref[halo - 1:halo - 1 + ts, :]
         + cw_ref[2:3, :] * cu)
    o_ref[...] = (b * z).astype(BF16)
    cu_ref[0:halo, :] = cu_ref[ts:ts + halo, :]


def _conv_front(x, w_in, conv_w, batch, seq):
    n = x.shape[0]
    ts = CONV_ROWS
    per_b = seq // ts
    rows = lambda w: pl.BlockSpec((ts, w), lambda b, j: (b * per_b + j, 0))
    win = w_in.astype(BF16)
    return pl.pallas_call(
        _conv_kernel,
        grid=(batch, per_b),
        in_specs=[rows(D_MODEL), _const_spec(win.shape), _const_spec(conv_w.shape)],
        out_specs=rows(D_MODEL),
        out_shape=jax.ShapeDtypeStruct((n, D_MODEL), BF16),
        scratch_shapes=[pltpu.VMEM((SUBLANES + ts, D_MODEL), F32)],
        compiler_params=_params("parallel", "arbitrary"),
        name="conv_front",
    )(x, win, conv_w)


def kernel(x, p, positions, gla_w_in, gla_w_gate_up, gla_b_gate, gla_norm_g, gla_w_out,
           mla_w_in, mla_q_norm, mla_kv_norm, mla_w_uq, mla_w_ukv, mla_w_out,
           conv_w_in, conv_w, conv_w_out, ln_g, ln_b, mlp_w1, mlp_w2, ple_w_gate, ple_w_proj):
    batch, seq, d = x.shape
    n = batch * seq
    assert d == D_MODEL and seq % GLA_ROWS == 0 and seq % CONV_ROWS == 0 and seq % MLA_Q_TILE == 0
    xf = x.reshape(n, d)
    pos = positions.astype(F32).reshape(n)
    p_all = p.reshape(DEPTH * n, PLE_DIM)
    for i in range(DEPTH):
        j, kind = i // N_MIXERS, i % N_MIXERS
        if kind == 0:
            o = _gla_front(xf, gla_w_in[j], gla_w_gate_up[j], gla_b_gate[j], gla_norm_g[j], batch, seq)
            w_out = gla_w_out[j]
        elif kind == 1:
            o = _mla_front(xf, pos, mla_w_in[j], mla_q_norm[j], mla_kv_norm[j], mla_w_uq[j],
                           mla_w_ukv[j], batch, seq)
            w_out = mla_w_out[j]
        else:
            o = _conv_front(xf, conv_w_in[j], conv_w[j], batch, seq)
            w_out = conv_w_out[j]
        ln = jnp.concatenate([ln_g[i], ln_b[i]], axis=0)
        xf = _tail(o, xf, p_all, i, w_out.astype(BF16), mlp_w1[i].astype(BF16),
                   mlp_w2[i].astype(BF16), ple_w_gate[i].astype(BF16), ple_w_proj[i].astype(BF16), ln)
    return xf.reshape(batch, seq, d)
```

```python
import functools

import numpy as np
import jax
import jax.numpy as jnp
from jax.experimental import pallas as pl
from jax.experimental.pallas import tpu as pltpu

F32 = jnp.float32
BF16 = jnp.bfloat16

D_MODEL = 1024
DEPTH = 4
CHUNK = 64
N_MIXERS = 3
ALPHA = (2 * DEPTH) ** 0.25
LN_EPS = 1e-5
RMS_EPS = 1e-6
PLE_DIM = 256
D_FF = 4 * D_MODEL

GLA_HEADS = 4
GLA_DK = 128
GLA_DV = 256
GLA_GATE_RANK = 16
GLA_TAU = 16.0
GLA_HK = GLA_HEADS * GLA_DK
GLA_HV = GLA_HEADS * GLA_DV
CHUNK_SHIFT = CHUNK.bit_length() - 1
assert 1 << CHUNK_SHIFT == CHUNK
GLA_LEVELS = CHUNK_SHIFT

MLA_HEADS = 8
MLA_NOPE = 128
MLA_ROPE = 64
MLA_V = 128
MLA_Q_RANK = 256
MLA_KV_RANK = 256
MLA_QK_PAD = 256
ROPE_BASE = 10000.0

CONV_WIDTH = 3

LANES = 128
SUBLANES = 8
MXU_WIDTH = 256
VMEM_LIMIT_BYTES = 56 * 1024 * 1024

TAIL_ROWS = 512
TAIL_SUB_ROWS = 256
FF_CHUNK = 1024
GLA_ROWS = 512
TRI_ROWS = 256
MLA_PROJ_ROWS = 512
MLA_PROJ_SUB_ROWS = 256
ROPE_TABLE_ROWS = 512
MLA_Q_TILE = 256
CONV_ROWS = 512


def _params(*semantics):
    return pltpu.CompilerParams(dimension_semantics=semantics, vmem_limit_bytes=VMEM_LIMIT_BYTES)


def _const_spec(shape):
    return pl.BlockSpec(shape, lambda *_: (0,) * len(shape), pipeline_mode=pl.Buffered(1))


def _dot(a, b):
    return jnp.dot(a, b, preferred_element_type=F32)


def _dot_nt(a, b):
    return jax.lax.dot_general(a, b, (((1,), (1,)), ((), ())), preferred_element_type=F32)


def _dot_tn(a, b):
    return jax.lax.dot_general(a, b, (((0,), (0,)), ((), ())), preferred_element_type=F32)


def _layer_norm(x, g, b):
    mu = jnp.mean(x, axis=-1, keepdims=True)
    xc = x - mu
    var = jnp.mean(xc * xc, axis=-1, keepdims=True)
    return xc * jax.lax.rsqrt(var + LN_EPS) * g + b


def _rms_norm(x, g):
    return x * jax.lax.rsqrt(jnp.mean(x * x, axis=-1, keepdims=True) + RMS_EPS) * g


def _sigmoid(x):
    return 1.0 / (1.0 + jnp.exp(-x))


def _skewed(generators):
    live = list(enumerate(generators))
    t = 0
    while live:
        still = []
        for i, gen in live:
            if t < i:
                still.append((i, gen))
                continue
            try:
                next(gen)
                still.append((i, gen))
            except StopIteration:
                pass
        live = still
        t += 1


def _tail_kernel(o_ref, x_ref, p_ref, wout_ref, w1_ref, w2_ref, wg_ref, wp_ref, lng_ref, lnb_ref, out_ref):
    g0, g1 = lng_ref[0, 0:1, :], lng_ref[0, 1:2, :]
    b0, b1 = lnb_ref[0, 0:1, :], lnb_ref[0, 1:2, :]

    def stages(rows):
        h = _dot(o_ref[rows, :], wout_ref[0])
        yield
        x1 = _layer_norm(ALPHA * x_ref[rows, :] + h, g0, b0)
        x1b = x1.astype(BF16)
        yield
        acc = None
        for c in range(D_FF // FF_CHUNK):
            cols = slice(c * FF_CHUNK, (c + 1) * FF_CHUNK)
            hc = _dot(x1b, w1_ref[0, :, cols])
            hc = jnp.square(jnp.maximum(hc, 0.0)).astype(BF16)
            part = _dot(hc, w2_ref[0, cols, :])
            acc = part if acc is None else acc + part
            yield
        x2 = _layer_norm(ALPHA * x1 + acc, g1, b1)
        x2b = x2.astype(BF16)
        yield
        gate_in = _dot(x2b, wg_ref[0])
        proj = _dot(p_ref[rows, :].astype(BF16), wp_ref[0])
        yield
        out_ref[rows, :] = x2 + _sigmoid(gate_in) * proj

    _skewed([stages(slice(r, r + TAIL_SUB_ROWS)) for r in range(0, TAIL_ROWS, TAIL_SUB_ROWS)])


def _layer_spec(stack, layer):
    shape = (1,) + stack.shape[1:]
    return pl.BlockSpec(shape, lambda *_: (layer,) + (0,) * (len(shape) - 1), pipeline_mode=pl.Buffered(1))


def _tail(o, x, p_all, layer, wout_all, mixer_index, w1_all, w2_all, wg_all, wp_all, lng_all, lnb_all):
    n = x.shape[0]
    tm = TAIL_ROWS
    rows = lambda w: pl.BlockSpec((tm, w), lambda i: (i, 0))
    p_rows = pl.BlockSpec((tm, PLE_DIM), lambda i: (layer * (n // tm) + i, 0))
    return pl.pallas_call(
        _tail_kernel,
        grid=(n // tm,),
        in_specs=[rows(D_MODEL), rows(D_MODEL), p_rows,
                  _layer_spec(wout_all, mixer_index), _layer_spec(w1_all, layer), _layer_spec(w2_all, layer),
                  _layer_spec(wg_all, layer), _layer_spec(wp_all, layer),
                  _layer_spec(lng_all, layer), _layer_spec(lnb_all, layer)],
        out_specs=rows(D_MODEL),
        out_shape=jax.ShapeDtypeStruct((n, D_MODEL), F32),
        compiler_params=_params("parallel"),
        name="tail",
    )(o, x, p_all, wout_all, w1_all, w2_all, wg_all, wp_all, lng_all, lnb_all)


def _split3_bf16(a):
    hi = a.astype(BF16)
    r1 = a - hi.astype(F32)
    mid = r1.astype(BF16)
    lo = (r1 - mid.astype(F32)).astype(BF16)
    return hi, mid, lo


def _gla_project_items(x_ref, xb_ref, wqkvr_ref, wlr_ref, wgu_ref, bg_ref, tri_ref,
                       q_ref, k_ref, v_ref, r_ref, l_ref):
    w = MXU_WIDTH

    def cast_x():
        xb_ref[...] = x_ref[...].astype(BF16)

    def column_tile(dst_ref, dst_col, w_col, finish):
        def item():
            y = _dot(xb_ref[...], wqkvr_ref[0, :, w_col:w_col + w])
            dst_ref[:, dst_col:dst_col + w] = finish(y)
        return item

    def log_decay():
        g_lr = _dot(xb_ref[...], wlr_ref[0])
        z = _dot(g_lr.astype(BF16), wgu_ref[0]) + bg_ref[0]
        log_a = (jnp.minimum(z, 0.0) - jnp.log1p(jnp.exp(-jnp.abs(z)))) * (1.0 / GLA_TAU)
        tri = tri_ref[...]
        for r0 in range(0, GLA_ROWS, TRI_ROWS):
            hi, mid, lo = _split3_bf16(log_a[r0:r0 + TRI_ROWS, :])
            l_ref[r0:r0 + TRI_ROWS, :] = (_dot(tri, lo) + _dot(tri, mid)) + _dot(tri, hi)

    items = [cast_x, log_decay]
    groups = [(q_ref, GLA_HK, lambda y: y * (GLA_DK ** -0.5)), (k_ref, GLA_HK, lambda y: y),
              (v_ref, GLA_HV, lambda y: y.astype(BF16)), (r_ref, GLA_HV, lambda y: y * _sigmoid(y))]
    w_col = 0
    for dst_ref, width, finish in groups:
        for dst_col in range(0, width, w):
            items.append(column_tile(dst_ref, dst_col, w_col, finish))
            w_col += w
    return items


def _level_ref_rows(L, level):
    C, dk = L.shape
    m = 1 << level
    if m >= SUBLANES:
        parts = [jnp.broadcast_to(L[b + m - 1:b + m, :], (2 * m, dk)) for b in range(0, C, 2 * m)]
        return parts[0] if len(parts) == 1 else jnp.concatenate(parts, axis=0)
    if m == 1:
        row = jax.lax.broadcasted_iota(jnp.int32, (C, dk), 0)
        return jnp.where((row & 1) == 1, pltpu.roll(L, shift=1, axis=0), L)
    tiles = (C // SUBLANES, SUBLANES, dk)
    L3 = L.reshape(tiles)
    pick = lambda i: jnp.broadcast_to(L3[:, i:i + 1, :], tiles)
    if m == 4:
        ref = pick(3)
    else:
        sub = jax.lax.broadcasted_iota(jnp.int32, tiles, 1)
        ref = jnp.where(sub < 4, pick(1), pick(5))
    return ref.reshape(C, dk)


def _gla_chunk(qs, ks, vs, Ls, states_t, pair_level, tick):
    C = CHUNK
    heads = range(len(qs))
    l_ends = [L[C - 1:C, :] for L in Ls]
    os = [_dot_nt((qs[h] * jnp.exp(Ls[h])).astype(BF16), states_t[h].astype(BF16)) for h in heads]
    k_decs = [(ks[h] * jnp.exp(l_ends[h] - Ls[h])).astype(BF16) for h in heads]
    new_states_t = [states_t[h] * jnp.exp(l_ends[h]) + _dot_tn(vs[h], k_decs[h]) for h in heads]
    tick()

    scores = [jnp.where(pair_level < 0, jnp.sum(qs[h] * ks[h], axis=1, keepdims=True), 0.0)
              for h in heads]
    for level in range(GLA_LEVELS):
        for h in heads:
            f = jnp.exp(-jnp.abs(Ls[h] - _level_ref_rows(Ls[h], level)))
            s_level = _dot_nt((qs[h] * f).astype(BF16), (ks[h] * f).astype(BF16))
            scores[h] = jnp.where(pair_level == level, s_level, scores[h])
        tick()

    os = [os[h] + _dot(scores[h].astype(BF16), vs[h]) for h in heads]
    return os, new_states_t


def _gla_scan_tile(q_ref, k_ref, v_ref, r_ref, l_ref, g_ref, lvl_ref, o_ref, state_ref, tick):
    g = g_ref[0]
    pair_level = lvl_ref[...]

    for c in range(GLA_ROWS // CHUNK):
        rows = slice(c * CHUNK, (c + 1) * CHUNK)
        kcs = [slice(h * GLA_DK, (h + 1) * GLA_DK) for h in range(GLA_HEADS)]
        vcs = [slice(h * GLA_DV, (h + 1) * GLA_DV) for h in range(GLA_HEADS)]
        os, new_states = _gla_chunk([q_ref[rows, kc] for kc in kcs], [k_ref[rows, kc] for kc in kcs],
                                    [v_ref[rows, vc] for vc in vcs], [l_ref[rows, kc] for kc in kcs],
                                    [state_ref[h] for h in range(GLA_HEADS)], pair_level, tick)
        for h in range(GLA_HEADS):
            state_ref[h] = new_states[h]
            o_ref[rows, vcs[h]] = (_rms_norm(os[h], g) * r_ref[rows, vcs[h]]).astype(BF16)
        tick()


def _gla_front_kernel(tiles_per_seq, x_ref, wqkvr_ref, wlr_ref, wgu_ref, bg_ref, tri_ref, g_ref, lvl_ref,
                      o_ref, state_ref, xb_ref, *slots):
    step = pl.program_id(0)
    slot_a, slot_b = slots[:5], slots[5:]

    @pl.when(step == 0)
    def _():
        for ref in slot_b:
            ref[...] = jnp.zeros_like(ref)

    @pl.when((step == 0) | (jax.lax.rem(step - 1, tiles_per_seq) == 0))
    def _():
        state_ref[...] = jnp.zeros_like(state_ref)

    def body(dst, src):
        items = _gla_project_items(x_ref, xb_ref, wqkvr_ref, wlr_ref, wgu_ref, bg_ref, tri_ref, *dst)
        n_items, n_ticks = len(items), (GLA_ROWS // CHUNK) * (GLA_LEVELS + 2)
        done = [0, 0]

        def tick():
            done[0] += 1
            while done[1] < n_items and done[1] * (n_ticks - 2) < done[0] * n_items:
                items[done[1]]()
                done[1] += 1

        items[0]()
        done[1] = 1
        _gla_scan_tile(*src, g_ref, lvl_ref, o_ref, state_ref, tick)
        assert done == [n_ticks, n_items]

    parity = jax.lax.rem(step, 2)
    pl.when(parity == 0)(lambda: body(slot_a, slot_b))
    pl.when(parity == 1)(lambda: body(slot_b, slot_a))


def _gla_weights(w_in, w_gate_up, b_gate, norm_g):
    main = 2 * GLA_HK + GLA_HV + D_MODEL
    w_all = w_in.astype(BF16)
    wlr = jnp.pad(w_in[:, :, main:], ((0, 0), (0, 0), (0, LANES - GLA_GATE_RANK))).astype(BF16)
    wgu = jnp.pad(w_gate_up, ((0, 0), (0, LANES - GLA_GATE_RANK), (0, 0))).astype(BF16)
    return w_all, wlr, wgu, b_gate[:, None, :], norm_g[:, None, :]


def _gla_front(x, weights, layer, batch, seq):
    n = x.shape[0]
    t = GLA_ROWS
    n_tiles = n // t
    wqkvr, wlr, wgu, bg, g = weights
    idx = np.arange(TRI_ROWS)
    tri = ((idx[:, None] // CHUNK == idx[None, :] // CHUNK) & (idx[None, :] <= idx[:, None]))
    tri = jnp.asarray(tri, BF16)
    idx = np.arange(CHUNK)
    differ = idx[:, None] ^ idx[None, :]
    pair_level = jnp.asarray(np.floor(np.log2(np.maximum(differ, 1))).astype(np.int32) - (differ == 0))
    slot = [pltpu.VMEM((t, GLA_HK), F32), pltpu.VMEM((t, GLA_HK), F32), pltpu.VMEM((t, GLA_HV), BF16),
            pltpu.VMEM((t, GLA_HV), F32), pltpu.VMEM((t, GLA_HK), F32)]
    return pl.pallas_call(
        functools.partial(_gla_front_kernel, seq // t),
        grid=(n_tiles + 1,),
        in_specs=[pl.BlockSpec((t, D_MODEL), lambda s: (jnp.minimum(s, n_tiles - 1), 0)),
                  _layer_spec(wqkvr, layer), _layer_spec(wlr, layer), _layer_spec(wgu, layer),
                  _layer_spec(bg, layer), _const_spec(tri.shape), _layer_spec(g, layer),
                  _const_spec(pair_level.shape)],
        out_specs=pl.BlockSpec((t, GLA_HV), lambda s: (jnp.maximum(s - 1, 0), 0)),
        out_shape=jax.ShapeDtypeStruct((n, GLA_HV), BF16),
        scratch_shapes=[pltpu.VMEM((GLA_HEADS, GLA_DV, GLA_DK), F32), pltpu.VMEM((t, D_MODEL), BF16)]
                       + slot + slot,
        compiler_params=_params("arbitrary"),
        name="gla_front",
    )(x, wqkvr, wlr, wgu, bg, tri, g, pair_level)


def _rope_table_kernel(pos_ref, freq_ref, cos_ref, sin_ref):
    ang = pos_ref[...] * freq_ref[...]
    cos_ref[...] = jnp.cos(ang)
    sin_ref[...] = jnp.sin(ang)


def _rope_tables(pos):
    n, half = pos.shape[0], MLA_ROPE // 2
    per_row = LANES // half
    inv_freq = ROPE_BASE ** (-jnp.arange(0, half, dtype=F32) * (2.0 / MLA_ROPE))
    pos_t = pos.reshape(n // MLA_PROJ_ROWS, per_row, MLA_PROJ_ROWS // per_row).transpose(0, 2, 1)
    pos_d = jnp.repeat(pos_t.reshape(n // per_row, per_row), half, axis=1)
    freq_d = jnp.tile(inv_freq, per_row)[None, :]
    rows_d = n // per_row
    tm = min(rows_d, ROPE_TABLE_ROWS)
    blk = pl.BlockSpec((tm, LANES), lambda i: (i, 0))
    cos_d, sin_d = pl.pallas_call(
        _rope_table_kernel,
        grid=(rows_d // tm,),
        in_specs=[blk, _const_spec(freq_d.shape)],
        out_specs=[blk, blk],
        out_shape=[jax.ShapeDtypeStruct((rows_d, LANES), F32)] * 2,
        compiler_params=_params("parallel"),
        name="rope_tables",
    )(pos_d, freq_d)
    return cos_d, sin_d


def _mla_proj_kernel(x_ref, cos_ref, sin_ref, win_ref, qn_ref, kvn_ref,
                     wq_ref, wk_ref, wvt_ref, q_ref, k_ref, vt_ref):
    H, R = MLA_HEADS, MLA_Q_RANK
    half = MLA_ROPE // 2
    group_rows = MLA_PROJ_ROWS // (LANES // half)
    scale = (MLA_NOPE + MLA_ROPE) ** -0.5

    def stages(r0):
        r1 = r0 + MLA_PROJ_SUB_ROWS
        c = _dot(x_ref[r0:r1, :].astype(BF16), win_ref[...])
        yield
        cq = _rms_norm(c[:, 0:R], qn_ref[...]).astype(BF16)
        ckv = _rms_norm(c[:, R:R + MLA_KV_RANK], kvn_ref[...]).astype(BF16)
        pad = jnp.zeros((group_rows, LANES - MLA_ROPE), F32)
        cos_blocks, sin_blocks = [], []
        for j in range(r0 // group_rows, r1 // group_rows):
            c32 = cos_ref[:, j * half:(j + 1) * half]
            s32 = sin_ref[:, j * half:(j + 1) * half]
            cos_blocks.append(jnp.concatenate([c32, c32, pad], axis=1))
            sin_blocks.append(jnp.concatenate([-s32, s32, pad], axis=1))
        cos = jnp.concatenate(cos_blocks, axis=0)
        sin = jnp.concatenate(sin_blocks, axis=0)
        k_rope = (c[:, 2 * R:2 * R + LANES] * cos + c[:, 2 * R + LANES:2 * R + 2 * LANES] * sin).astype(BF16)
        yield
        qq = _dot(cq, wq_ref[...])
        k_nope = _dot(ckv, wk_ref[...])
        vt_ref[:, r0:r1] = _dot_nt(wvt_ref[...], ckv).astype(BF16)
        yield
        for h in range(H):
            lo = h * LANES
            q_nope = qq[:, lo:lo + LANES]
            q_r = qq[:, H * LANES + lo:H * LANES + lo + LANES]
            q_sw = qq[:, 2 * H * LANES + lo:2 * H * LANES + lo + LANES]
            base = h * MLA_QK_PAD
            q_ref[r0:r1, base:base + LANES] = (q_nope * scale).astype(BF16)
            q_ref[r0:r1, base + LANES:base + 2 * LANES] = ((q_r * cos + q_sw * sin) * scale).astype(BF16)
            k_ref[r0:r1, base:base + LANES] = k_nope[:, lo:lo + LANES].astype(BF16)
            k_ref[r0:r1, base + LANES:base + 2 * LANES] = k_rope

    _skewed([stages(r0) for r0 in range(0, MLA_PROJ_ROWS, MLA_PROJ_SUB_ROWS)])


def _mla_proj(x, cos, sin, win, qn, kvn, wq, wk, wvt, seq):
    n = x.shape[0]
    tm = MLA_PROJ_ROWS
    per_b = seq // tm
    rows = lambda w: pl.BlockSpec((tm, w), lambda i: (i, 0))
    qk_w = MLA_HEADS * MLA_QK_PAD
    hv = MLA_HEADS * MLA_V
    table = pl.BlockSpec((tm // (LANES // (MLA_ROPE // 2)), LANES), lambda i: (i, 0))
    return pl.pallas_call(
        _mla_proj_kernel,
        grid=(n // tm,),
        in_specs=[rows(D_MODEL), table, table, _const_spec(win.shape), _const_spec(qn.shape), _const_spec(kvn.shape),
                  _const_spec(wq.shape), _const_spec(wk.shape), _const_spec(wvt.shape)],
        out_specs=[rows(qk_w), rows(qk_w),
                   pl.BlockSpec((hv, tm), lambda i: (i // per_b, i % per_b))],
        out_shape=[jax.ShapeDtypeStruct((n, qk_w), BF16), jax.ShapeDtypeStruct((n, qk_w), BF16),
                   jax.ShapeDtypeStruct((n // seq * hv, seq), BF16)],
        compiler_params=_params("parallel"),
        name="mla_proj",
    )(x, cos, sin, win, qn, kvn, wq, wk, wvt)


def _mla_attn_kernel(q_ref, k_ref, vt_ref, o_ref):
    seq = q_ref.shape[0]
    tq = MLA_Q_TILE
    key = jax.lax.broadcasted_iota(jnp.int32, (tq, tq), 0)
    qry = jax.lax.broadcasted_iota(jnp.int32, (tq, tq), 1)
    visible = jnp.right_shift(key, CHUNK_SHIFT) <= jnp.right_shift(qry, CHUNK_SHIFT)

    def scores(qi):
        lo, hi = qi * tq, (qi + 1) * tq
        q = q_ref[lo:hi, :]
        s_diag = jnp.where(visible, _dot_nt(k_ref[lo:hi, :], q), -jnp.inf)
        s_past = _dot_nt(k_ref[0:lo, :], q) if qi else None
        return s_diag, s_past

    def finish(qi, s_diag, s_past):
        lo, hi = qi * tq, (qi + 1) * tq
        m = jnp.max(s_diag, axis=0, keepdims=True)
        if qi:
            m = jnp.maximum(m, jnp.max(s_past, axis=0, keepdims=True))
        p = jnp.exp(s_diag - m)
        den = jnp.sum(p, axis=0, keepdims=True)
        acc = _dot(vt_ref[:, lo:hi], p.astype(BF16))
        if qi:
            p = jnp.exp(s_past - m)
            den = den + jnp.sum(p, axis=0, keepdims=True)
            acc = acc + _dot(vt_ref[:, 0:lo], p.astype(BF16))
        o_ref[lo:hi, :] = jnp.transpose(acc / den).astype(BF16)

    order = list(range(seq // tq))[::-1]
    ahead = scores(order[0])
    for i, qi in enumerate(order):
        current = ahead
        if i + 1 < len(order):
            ahead = scores(order[i + 1])
        finish(qi, *current)


def _mla_attn(q, k, vt, batch, seq):
    n = q.shape[0]
    blk = lambda w: pl.BlockSpec((seq, w), lambda b, h: (b, h))
    return pl.pallas_call(
        _mla_attn_kernel,
        grid=(batch, MLA_HEADS),
        in_specs=[blk(MLA_QK_PAD), blk(MLA_QK_PAD),
                  pl.BlockSpec((MLA_V, seq), lambda b, h: (b * MLA_HEADS + h, 0))],
        out_specs=blk(MLA_V),
        out_shape=jax.ShapeDtypeStruct((n, MLA_HEADS * MLA_V), BF16),
        compiler_params=_params("parallel", "parallel"),
        name="mla_attn",
    )(q, k, vt)


def _mla_front(x, pos, w_in, q_norm, kv_norm, w_uq, w_ukv, batch, seq):
    H, half = MLA_HEADS, MLA_ROPE // 2
    pad = LANES - MLA_ROPE

    def rope_cols(w, swap):
        a, b = w[..., :half], w[..., half:]
        if swap:
            a, b = b, a
        return jnp.concatenate([a, b, jnp.zeros(w.shape[:-1] + (pad,), w.dtype)], axis=-1)

    kr_w = w_in[:, MLA_Q_RANK + MLA_KV_RANK:]
    win = jnp.concatenate([w_in[:, :MLA_Q_RANK + MLA_KV_RANK], rope_cols(kr_w, False),
                           rope_cols(kr_w, True)], axis=1).astype(BF16)
    wq3 = w_uq.reshape(MLA_Q_RANK, H, MLA_NOPE + MLA_ROPE)
    wq = jnp.concatenate([wq3[:, :, :MLA_NOPE].reshape(MLA_Q_RANK, H * LANES),
                          rope_cols(wq3[:, :, MLA_NOPE:], False).reshape(MLA_Q_RANK, H * LANES),
                          rope_cols(wq3[:, :, MLA_NOPE:], True).reshape(MLA_Q_RANK, H * LANES)],
                         axis=1).astype(BF16)
    wkv3 = w_ukv.reshape(MLA_KV_RANK, H, MLA_NOPE + MLA_V)
    wk = wkv3[:, :, :MLA_NOPE].reshape(MLA_KV_RANK, H * MLA_NOPE).astype(BF16)
    wvt = wkv3[:, :, MLA_NOPE:].reshape(MLA_KV_RANK, H * MLA_V).T.astype(BF16)
    cos, sin = _rope_tables(pos)
    q, k, v = _mla_proj(x, cos, sin, win, q_norm[None, :], kv_norm[None, :], wq, wk, wvt, seq)
    return _mla_attn(q, k, v, batch, seq)


def _conv_kernel(x_ref, win_ref, cw_ref, o_ref, cu_ref, xb_ref):
    ts = x_ref.shape[0]
    halo = SUBLANES

    @pl.when(pl.program_id(1) == 0)
    def _():
        cu_ref[0:halo, :] = jnp.zeros((halo, D_MODEL), F32)

    cw = cw_ref[0]

    xb_ref[...] = x_ref[...].astype(BF16)

    def stages(c0):
        cols = slice(c0, c0 + MXU_WIDTH)
        xb = xb_ref[...]
        b = _dot(xb, win_ref[0, :, c0:c0 + MXU_WIDTH])
        cu = (_dot(xb, win_ref[0, :, D_MODEL + c0:D_MODEL + c0 + MXU_WIDTH])
              * _dot(xb, win_ref[0, :, 2 * D_MODEL + c0:2 * D_MODEL + c0 + MXU_WIDTH]))
        cu_ref[halo:halo + ts, cols] = cu
        yield
        z = (cw[0:1, cols] * cu_ref[halo - 2:halo - 2 + ts, cols]
             + cw[1:2, cols] * cu_ref[halo - 1:halo - 1 + ts, cols]
             + cw[2:3, cols] * cu)
        o_ref[:, cols] = (b * z).astype(BF16)
        cu_ref[0:halo, cols] = cu_ref[ts:ts + halo, cols]

    _skewed([stages(c0) for c0 in range(0, D_MODEL, MXU_WIDTH)])


def _conv_front(x, w_in_all, conv_w_all, layer, batch, seq):
    n = x.shape[0]
    ts = CONV_ROWS
    per_b = seq // ts
    rows = lambda w: pl.BlockSpec((ts, w), lambda b, j: (b * per_b + j, 0))
    return pl.pallas_call(
        _conv_kernel,
        grid=(batch, per_b),
        in_specs=[rows(D_MODEL), _layer_spec(w_in_all, layer), _layer_spec(conv_w_all, layer)],
        out_specs=rows(D_MODEL),
        out_shape=jax.ShapeDtypeStruct((n, D_MODEL), BF16),
        scratch_shapes=[pltpu.VMEM((SUBLANES + ts, D_MODEL), F32), pltpu.VMEM((ts, D_MODEL), BF16)],
        compiler_params=_params("parallel", "arbitrary"),
        name="conv_front",
    )(x, w_in_all, conv_w_all)


def kernel(x, p, positions, gla_w_in, gla_w_gate_up, gla_b_gate, gla_norm_g, gla_w_out,
           mla_w_in, mla_q_norm, mla_kv_norm, mla_w_uq, mla_w_ukv, mla_w_out,
           conv_w_in, conv_w, conv_w_out, ln_g, ln_b, mlp_w1, mlp_w2, ple_w_gate, ple_w_proj):
    batch, seq, d = x.shape
    n = batch * seq
    assert d == D_MODEL and seq % GLA_ROWS == 0 and seq % CONV_ROWS == 0 and seq % MLA_Q_TILE == 0
    xf = x.reshape(n, d)
    pos = positions.astype(F32).reshape(n)
    p_all = p.reshape(DEPTH * n, PLE_DIM)
    w_out_all = [w.astype(BF16) for w in (gla_w_out, mla_w_out, conv_w_out)]
    w1_all, w2_all = mlp_w1.astype(BF16), mlp_w2.astype(BF16)
    wg_all, wp_all = ple_w_gate.astype(BF16), ple_w_proj.astype(BF16)
    gla_weights = _gla_weights(gla_w_in, gla_w_gate_up, gla_b_gate, gla_norm_g)
    conv_w_in_all = conv_w_in.astype(BF16)
    for i in range(DEPTH):
        j, kind = i // N_MIXERS, i % N_MIXERS
        if kind == 0:
            o = _gla_front(xf, gla_weights, j, batch, seq)
        elif kind == 1:
            o = _mla_front(xf, pos, mla_w_in[j], mla_q_norm[j], mla_kv_norm[j], mla_w_uq[j],
                           mla_w_ukv[j], batch, seq)
        else:
            o = _conv_front(xf, conv_w_in_all, conv_w, j, batch, seq)
        xf = _tail(o, xf, p_all, i, w_out_all[kind], j, w1_all, w2_all, wg_all, wp_all, ln_g, ln_b)
    return xf.reshape(batch, seq, d)
```

```python
import functools

import numpy as np
import jax
import jax.numpy as jnp
from jax.experimental import pallas as pl
from jax.experimental.pallas import tpu as pltpu

F32 = jnp.float32
BF16 = jnp.bfloat16

D_MODEL = 1024
DEPTH = 4
CHUNK = 64
N_MIXERS = 3
ALPHA = (2 * DEPTH) ** 0.25
LN_EPS = 1e-5
RMS_EPS = 1e-6
PLE_DIM = 256
D_FF = 4 * D_MODEL

GLA_HEADS = 4
GLA_DK = 128
GLA_DV = 256
GLA_GATE_RANK = 16
GLA_TAU = 16.0
GLA_HK = GLA_HEADS * GLA_DK
GLA_HV = GLA_HEADS * GLA_DV
CHUNK_SHIFT = CHUNK.bit_length() - 1
assert 1 << CHUNK_SHIFT == CHUNK
GLA_LEVELS = CHUNK_SHIFT

MLA_HEADS = 8
MLA_NOPE = 128
MLA_ROPE = 64
MLA_V = 128
MLA_Q_RANK = 256
MLA_KV_RANK = 256
MLA_QK_PAD = 256
ROPE_BASE = 10000.0

CONV_WIDTH = 3

LANES = 128
SUBLANES = 8
MXU_WIDTH = 256
VMEM_LIMIT_BYTES = 56 * 1024 * 1024

TAIL_ROWS = 512
TAIL_SUB_ROWS = 256
FF_CHUNK = 1024
GLA_ROWS = 512
GLA_ITEMS_UP_FRONT = 1
TRI_ROWS = 256
MLA_PROJ_ROWS = 512
MLA_PROJ_SUB_ROWS = 256
ROPE_TABLE_ROWS = 512
MLA_Q_TILE = 256
MLA_HEADS_PER_STEP = 2
CONV_ROWS = 512


def _params(*semantics):
    return pltpu.CompilerParams(dimension_semantics=semantics, vmem_limit_bytes=VMEM_LIMIT_BYTES)


def _const_spec(shape):
    return pl.BlockSpec(shape, lambda *_: (0,) * len(shape), pipeline_mode=pl.Buffered(1))


def _dot(a, b):
    return jnp.dot(a, b, preferred_element_type=F32)


def _dot_nt(a, b):
    return jax.lax.dot_general(a, b, (((1,), (1,)), ((), ())), preferred_element_type=F32)


def _dot_tn(a, b):
    return jax.lax.dot_general(a, b, (((0,), (0,)), ((), ())), preferred_element_type=F32)


def _layer_norm(x, g, b):
    mu = jnp.mean(x, axis=-1, keepdims=True)
    xc = x - mu
    var = jnp.mean(xc * xc, axis=-1, keepdims=True)
    return xc * jax.lax.rsqrt(var + LN_EPS) * g + b


def _rms_norm(x, g):
    return x * jax.lax.rsqrt(jnp.mean(x * x, axis=-1, keepdims=True) + RMS_EPS) * g


def _sigmoid(x):
    return 1.0 / (1.0 + jnp.exp(-x))


def _skewed(generators):
    live = list(enumerate(generators))
    t = 0
    while live:
        still = []
        for i, gen in live:
            if t < i:
                still.append((i, gen))
                continue
            try:
                next(gen)
                still.append((i, gen))
            except StopIteration:
                pass
        live = still
        t += 1


def _tail_kernel(n_casts, o_ref, x_ref, p_ref, wout_ref, w1_ref, w2_ref, wg_ref, wp_ref, lng_ref, lnb_ref,
                 *rest):
    cast_src, out_ref, cast_dst = rest[:n_casts], rest[n_casts], rest[n_casts + 1:]
    for src, dst in zip(cast_src, cast_dst):
        dst[...] = src[...].astype(BF16)

    g0, g1 = lng_ref[0, 0:1, :], lng_ref[0, 1:2, :]
    b0, b1 = lnb_ref[0, 0:1, :], lnb_ref[0, 1:2, :]

    def stages(rows):
        h = _dot(o_ref[rows, :], wout_ref[0])
        yield
        x1 = _layer_norm(ALPHA * x_ref[rows, :] + h, g0, b0)
        x1b = x1.astype(BF16)
        yield
        acc = None
        for c in range(D_FF // FF_CHUNK):
            cols = slice(c * FF_CHUNK, (c + 1) * FF_CHUNK)
            hc = _dot(x1b, w1_ref[0, :, cols])
            hc = jnp.square(jnp.maximum(hc, 0.0)).astype(BF16)
            part = _dot(hc, w2_ref[0, cols, :])
            acc = part if acc is None else acc + part
            yield
        x2 = _layer_norm(ALPHA * x1 + acc, g1, b1)
        x2b = x2.astype(BF16)
        yield
        gate_in = _dot(x2b, wg_ref[0])
        proj = _dot(p_ref[rows, :].astype(BF16), wp_ref[0])
        yield
        out_ref[rows, :] = x2 + _sigmoid(gate_in) * proj

    _skewed([stages(slice(r, r + TAIL_SUB_ROWS)) for r in range(0, TAIL_ROWS, TAIL_SUB_ROWS)])


def _layer_spec(stack, layer):
    shape = (1,) + stack.shape[1:]
    return pl.BlockSpec(shape, lambda *_: (layer,) + (0,) * (len(shape) - 1), pipeline_mode=pl.Buffered(1))


def _tail(o, x, p_all, layer, wout, w1, w2, wg, wp_all, lng_all, lnb_all, casts):
    n = x.shape[0]
    tm = TAIL_ROWS
    steps = n // tm
    rows = lambda w: pl.BlockSpec((tm, w), lambda i: (i, 0))
    p_rows = pl.BlockSpec((tm, PLE_DIM), lambda i: (layer * steps + i, 0))
    cast_in, cast_out, cast_shapes = [], [], []
    for stack, index in casts:
        _, r, c = stack.shape
        slab = r // steps
        assert slab * steps == r and slab % (2 * SUBLANES) == 0
        cast_in.append(pl.BlockSpec((1, slab, c), functools.partial(lambda index, i: (index, i, 0), index)))
        cast_out.append(pl.BlockSpec((1, slab, c), lambda i: (0, i, 0)))
        cast_shapes.append(jax.ShapeDtypeStruct((1, r, c), BF16))
    outs = pl.pallas_call(
        functools.partial(_tail_kernel, len(casts)),
        grid=(steps,),
        in_specs=[rows(D_MODEL), rows(D_MODEL), p_rows,
                  _layer_spec(*wout), _layer_spec(*w1), _layer_spec(*w2), _layer_spec(*wg),
                  _layer_spec(wp_all, layer), _layer_spec(lng_all, layer), _layer_spec(lnb_all, layer)]
                 + cast_in,
        out_specs=[rows(D_MODEL)] + cast_out,
        out_shape=[jax.ShapeDtypeStruct((n, D_MODEL), F32)] + cast_shapes,
        compiler_params=_params("parallel"),
        name="tail",
    )(o, x, p_all, wout[0], w1[0], w2[0], wg[0], wp_all, lng_all, lnb_all, *[s for s, _ in casts])
    return outs[0], outs[1:]


def _split3_bf16(a):
    hi = a.astype(BF16)
    r1 = a - hi.astype(F32)
    mid = r1.astype(BF16)
    lo = (r1 - mid.astype(F32)).astype(BF16)
    return hi, mid, lo


def _gla_project_items(x_ref, xb_ref, wqkvr_ref, wlr_ref, wgu_ref, bg_ref, tri_ref,
                       q_ref, k_ref, v_ref, r_ref, l_ref):
    w = MXU_WIDTH

    def cast_x():
        xb_ref[...] = x_ref[...].astype(BF16)

    def column_tile(dst_ref, dst_col, w_col, finish):
        def item():
            y = _dot(xb_ref[...], wqkvr_ref[0, :, w_col:w_col + w])
            dst_ref[:, dst_col:dst_col + w] = finish(y)
        return item

    def log_decay():
        g_lr = _dot(xb_ref[...], wlr_ref[0])
        z = _dot(g_lr.astype(BF16), wgu_ref[0]) + bg_ref[0]
        log_a = (jnp.minimum(z, 0.0) - jnp.log1p(jnp.exp(-jnp.abs(z)))) * (1.0 / GLA_TAU)
        tri = tri_ref[...]
        for r0 in range(0, GLA_ROWS, TRI_ROWS):
            hi, mid, lo = _split3_bf16(log_a[r0:r0 + TRI_ROWS, :])
            l_ref[r0:r0 + TRI_ROWS, :] = (_dot(tri, lo) + _dot(tri, mid)) + _dot(tri, hi)

    items = [cast_x, log_decay]
    groups = [(q_ref, GLA_HK, lambda y: y * (GLA_DK ** -0.5)), (k_ref, GLA_HK, lambda y: y),
              (v_ref, GLA_HV, lambda y: y.astype(BF16)), (r_ref, GLA_HV, lambda y: y * _sigmoid(y))]
    w_col = 0
    for dst_ref, width, finish in groups:
        for dst_col in range(0, width, w):
            items.append(column_tile(dst_ref, dst_col, w_col, finish))
            w_col += w
    return items


def _level_ref_rows(L, level):
    C, dk = L.shape
    m = 1 << level
    if m >= SUBLANES:
        parts = [jnp.broadcast_to(L[b + m - 1:b + m, :], (2 * m, dk)) for b in range(0, C, 2 * m)]
        return parts[0] if len(parts) == 1 else jnp.concatenate(parts, axis=0)
    if m == 1:
        row = jax.lax.broadcasted_iota(jnp.int32, (C, dk), 0)
        return jnp.where((row & 1) == 1, pltpu.roll(L, shift=1, axis=0), L)
    tiles = (C // SUBLANES, SUBLANES, dk)
    L3 = L.reshape(tiles)
    pick = lambda i: jnp.broadcast_to(L3[:, i:i + 1, :], tiles)
    if m == 4:
        ref = pick(3)
    else:
        sub = jax.lax.broadcasted_iota(jnp.int32, tiles, 1)
        ref = jnp.where(sub < 4, pick(1), pick(5))
    return ref.reshape(C, dk)


def _gla_chunk(qs, ks, vs, Ls, states_t, pair_level, tick):
    C = CHUNK
    heads = range(len(qs))
    l_ends = [L[C - 1:C, :] for L in Ls]
    os = [_dot_nt((qs[h] * jnp.exp(Ls[h])).astype(BF16), states_t[h].astype(BF16)) for h in heads]
    k_decs = [(ks[h] * jnp.exp(l_ends[h] - Ls[h])).astype(BF16) for h in heads]
    new_states_t = [states_t[h] * jnp.exp(l_ends[h]) + _dot_tn(vs[h], k_decs[h]) for h in heads]
    tick()

    scores = [jnp.where(pair_level < 0, jnp.sum(qs[h] * ks[h], axis=1, keepdims=True), 0.0)
              for h in heads]
    for level in range(GLA_LEVELS):
        for h in heads:
            f = jnp.exp(-jnp.abs(Ls[h] - _level_ref_rows(Ls[h], level)))
            s_level = _dot_nt((qs[h] * f).astype(BF16), (ks[h] * f).astype(BF16))
            scores[h] = jnp.where(pair_level == level, s_level, scores[h])
        tick()

    os = [os[h] + _dot(scores[h].astype(BF16), vs[h]) for h in heads]
    return os, new_states_t


def _gla_scan_tile(q_ref, k_ref, v_ref, r_ref, l_ref, g_ref, lvl_ref, o_ref, state_ref, tick):
    g = g_ref[0]
    pair_level = lvl_ref[...]

    for c in range(GLA_ROWS // CHUNK):
        rows = slice(c * CHUNK, (c + 1) * CHUNK)
        kcs = [slice(h * GLA_DK, (h + 1) * GLA_DK) for h in range(GLA_HEADS)]
        vcs = [slice(h * GLA_DV, (h + 1) * GLA_DV) for h in range(GLA_HEADS)]
        os, new_states = _gla_chunk([q_ref[rows, kc] for kc in kcs], [k_ref[rows, kc] for kc in kcs],
                                    [v_ref[rows, vc] for vc in vcs], [l_ref[rows, kc] for kc in kcs],
                                    [state_ref[h] for h in range(GLA_HEADS)], pair_level, tick)
        for h in range(GLA_HEADS):
            state_ref[h] = new_states[h]
            o_ref[rows, vcs[h]] = (_rms_norm(os[h], g) * r_ref[rows, vcs[h]]).astype(BF16)
        tick()


def _gla_front_kernel(tiles_per_seq, x_ref, wqkvr_ref, wlr_ref, wgu_ref, bg_ref, tri_ref, g_ref, lvl_ref,
                      o_ref, state_ref, xb_ref, *slots):
    step = pl.program_id(0)
    slot_a, slot_b = slots[:5], slots[5:]

    @pl.when(step == 0)
    def _():
        for ref in slot_b:
            ref[...] = jnp.zeros_like(ref)

    @pl.when((step == 0) | (jax.lax.rem(step - 1, tiles_per_seq) == 0))
    def _():
        state_ref[...] = jnp.zeros_like(state_ref)

    def body(dst, src):
        items = _gla_project_items(x_ref, xb_ref, wqkvr_ref, wlr_ref, wgu_ref, bg_ref, tri_ref, *dst)
        n_items, n_ticks = len(items), (GLA_ROWS // CHUNK) * (GLA_LEVELS + 2)
        done = [0, 0]

        def tick():
            done[0] += 1
            while (done[1] < n_items and (done[1] - GLA_ITEMS_UP_FRONT) * (n_ticks - 2)
                   < done[0] * (n_items - GLA_ITEMS_UP_FRONT)):
                items[done[1]]()
                done[1] += 1

        for item in items[:GLA_ITEMS_UP_FRONT]:
            item()
        done[1] = GLA_ITEMS_UP_FRONT
        _gla_scan_tile(*src, g_ref, lvl_ref, o_ref, state_ref, tick)
        assert done == [n_ticks, n_items]

    parity = jax.lax.rem(step, 2)
    pl.when(parity == 0)(lambda: body(slot_a, slot_b))
    pl.when(parity == 1)(lambda: body(slot_b, slot_a))


def _gla_small_weights(w_in, w_gate_up, b_gate, norm_g):
    main = 2 * GLA_HK + GLA_HV + D_MODEL
    wlr = jnp.pad(w_in[:, :, main:], ((0, 0), (0, 0), (0, LANES - GLA_GATE_RANK))).astype(BF16)
    wgu = jnp.pad(w_gate_up, ((0, 0), (0, LANES - GLA_GATE_RANK), (0, 0))).astype(BF16)
    return wlr, wgu, b_gate[:, None, :], norm_g[:, None, :]


def _gla_front(x, w_in, small_weights, layer, batch, seq):
    n = x.shape[0]
    t = GLA_ROWS
    n_tiles = n // t
    wlr, wgu, bg, g = small_weights
    idx = np.arange(TRI_ROWS)
    tri = ((idx[:, None] // CHUNK == idx[None, :] // CHUNK) & (idx[None, :] <= idx[:, None]))
    tri = jnp.asarray(tri, BF16)
    idx = np.arange(CHUNK)
    differ = idx[:, None] ^ idx[None, :]
    pair_level = jnp.asarray(np.floor(np.log2(np.maximum(differ, 1))).astype(np.int32) - (differ == 0))
    slot = [pltpu.VMEM((t, GLA_HK), F32), pltpu.VMEM((t, GLA_HK), F32), pltpu.VMEM((t, GLA_HV), BF16),
            pltpu.VMEM((t, GLA_HV), F32), pltpu.VMEM((t, GLA_HK), F32)]
    return pl.pallas_call(
        functools.partial(_gla_front_kernel, seq // t),
        grid=(n_tiles + 1,),
        in_specs=[pl.BlockSpec((t, D_MODEL), lambda s: (jnp.minimum(s, n_tiles - 1), 0)),
                  _layer_spec(*w_in), _layer_spec(wlr, layer), _layer_spec(wgu, layer),
                  _layer_spec(bg, layer), _const_spec(tri.shape), _layer_spec(g, layer),
                  _const_spec(pair_level.shape)],
        out_specs=pl.BlockSpec((t, GLA_HV), lambda s: (jnp.maximum(s - 1, 0), 0)),
        out_shape=jax.ShapeDtypeStruct((n, GLA_HV), BF16),
        scratch_shapes=[pltpu.VMEM((GLA_HEADS, GLA_DV, GLA_DK), F32), pltpu.VMEM((t, D_MODEL), BF16)]
                       + slot + slot,
        compiler_params=_params("arbitrary"),
        name="gla_front",
    )(x, w_in[0], wlr, wgu, bg, tri, g, pair_level)


def _rope_table_kernel(pos_ref, freq_ref, cos_ref, sin_ref):
    ang = pos_ref[...] * freq_ref[...]
    cos_ref[...] = jnp.cos(ang)
    sin_ref[...] = jnp.sin(ang)


def _rope_tables(pos):
    n, half = pos.shape[0], MLA_ROPE // 2
    per_row = LANES // half
    inv_freq = ROPE_BASE ** (-jnp.arange(0, half, dtype=F32) * (2.0 / MLA_ROPE))
    pos_t = pos.reshape(n // MLA_PROJ_ROWS, per_row, MLA_PROJ_ROWS // per_row).transpose(0, 2, 1)
    pos_d = jnp.repeat(pos_t.reshape(n // per_row, per_row), half, axis=1)
    freq_d = jnp.tile(inv_freq, per_row)[None, :]
    rows_d = n // per_row
    tm = min(rows_d, ROPE_TABLE_ROWS)
    blk = pl.BlockSpec((tm, LANES), lambda i: (i, 0))
    cos_d, sin_d = pl.pallas_call(
        _rope_table_kernel,
        grid=(rows_d // tm,),
        in_specs=[blk, _const_spec(freq_d.shape)],
        out_specs=[blk, blk],
        out_shape=[jax.ShapeDtypeStruct((rows_d, LANES), F32)] * 2,
        compiler_params=_params("parallel"),
        name="rope_tables",
    )(pos_d, freq_d)
    return cos_d, sin_d


def _mla_proj_kernel(x_ref, cos_ref, sin_ref, win_ref, qn_ref, kvn_ref,
                     wq_ref, wk_ref, wvt_ref, q_ref, k_ref, vt_ref):
    H, R = MLA_HEADS, MLA_Q_RANK
    half = MLA_ROPE // 2
    group_rows = MLA_PROJ_ROWS // (LANES // half)
    scale = (MLA_NOPE + MLA_ROPE) ** -0.5

    def stages(r0):
        r1 = r0 + MLA_PROJ_SUB_ROWS
        c = _dot(x_ref[r0:r1, :].astype(BF16), win_ref[...])
        yield
        cq = _rms_norm(c[:, 0:R], qn_ref[...]).astype(BF16)
        ckv = _rms_norm(c[:, R:R + MLA_KV_RANK], kvn_ref[...]).astype(BF16)
        pad = jnp.zeros((group_rows, LANES - MLA_ROPE), F32)
        cos_blocks, sin_blocks = [], []
        for j in range(r0 // group_rows, r1 // group_rows):
            c32 = cos_ref[:, j * half:(j + 1) * half]
            s32 = sin_ref[:, j * half:(j + 1) * half]
            cos_blocks.append(jnp.concatenate([c32, c32, pad], axis=1))
            sin_blocks.append(jnp.concatenate([-s32, s32, pad], axis=1))
        cos = jnp.concatenate(cos_blocks, axis=0)
        sin = jnp.concatenate(sin_blocks, axis=0)
        k_rope = (c[:, 2 * R:2 * R + LANES] * cos + c[:, 2 * R + LANES:2 * R + 2 * LANES] * sin).astype(BF16)
        yield
        qq = _dot(cq, wq_ref[...])
        k_nope = _dot(ckv, wk_ref[...])
        vt_ref[:, r0:r1] = _dot_nt(wvt_ref[...], ckv).astype(BF16)
        yield
        for h in range(H):
            lo = h * LANES
            q_nope = qq[:, lo:lo + LANES]
            q_r = qq[:, H * LANES + lo:H * LANES + lo + LANES]
            q_sw = qq[:, 2 * H * LANES + lo:2 * H * LANES + lo + LANES]
            base = h * MLA_QK_PAD
            q_ref[r0:r1, base:base + LANES] = (q_nope * scale).astype(BF16)
            q_ref[r0:r1, base + LANES:base + 2 * LANES] = ((q_r * cos + q_sw * sin) * scale).astype(BF16)
            k_ref[r0:r1, base:base + LANES] = k_nope[:, lo:lo + LANES].astype(BF16)
            k_ref[r0:r1, base + LANES:base + 2 * LANES] = k_rope

    _skewed([stages(r0) for r0 in range(0, MLA_PROJ_ROWS, MLA_PROJ_SUB_ROWS)])


def _mla_proj(x, cos, sin, win, qn, kvn, wq, wk, wvt, seq):
    n = x.shape[0]
    tm = MLA_PROJ_ROWS
    per_b = seq // tm
    rows = lambda w: pl.BlockSpec((tm, w), lambda i: (i, 0))
    qk_w = MLA_HEADS * MLA_QK_PAD
    hv = MLA_HEADS * MLA_V
    table = pl.BlockSpec((tm // (LANES // (MLA_ROPE // 2)), LANES), lambda i: (i, 0))
    return pl.pallas_call(
        _mla_proj_kernel,
        grid=(n // tm,),
        in_specs=[rows(D_MODEL), table, table, _const_spec(win.shape), _const_spec(qn.shape), _const_spec(kvn.shape),
                  _const_spec(wq.shape), _const_spec(wk.shape), _const_spec(wvt.shape)],
        out_specs=[rows(qk_w), rows(qk_w),
                   pl.BlockSpec((hv, tm), lambda i: (i // per_b, i % per_b))],
        out_shape=[jax.ShapeDtypeStruct((n, qk_w), BF16), jax.ShapeDtypeStruct((n, qk_w), BF16),
                   jax.ShapeDtypeStruct((n // seq * hv, seq), BF16)],
        compiler_params=_params("parallel"),
        name="mla_proj",
    )(x, cos, sin, win, qn, kvn, wq, wk, wvt)


def _mla_attn_kernel(q_ref, k_ref, vt_ref, o_ref):
    seq = q_ref.shape[0]
    tq = MLA_Q_TILE
    key = jax.lax.broadcasted_iota(jnp.int32, (tq, tq), 0)
    qry = jax.lax.broadcasted_iota(jnp.int32, (tq, tq), 1)
    visible = jnp.right_shift(key, CHUNK_SHIFT) <= jnp.right_shift(qry, CHUNK_SHIFT)

    def scores(h, qi):
        lo, hi = qi * tq, (qi + 1) * tq
        qk = slice(h * MLA_QK_PAD, (h + 1) * MLA_QK_PAD)
        q = q_ref[lo:hi, qk]
        s_diag = jnp.where(visible, _dot_nt(k_ref[lo:hi, qk], q), -jnp.inf)
        s_past = _dot_nt(k_ref[0:lo, qk], q) if qi else None
        return s_diag, s_past

    def finish(h, qi, s_diag, s_past):
        lo, hi = qi * tq, (qi + 1) * tq
        dv = slice(h * MLA_V, (h + 1) * MLA_V)
        m = jnp.max(s_diag, axis=0, keepdims=True)
        if qi:
            m = jnp.maximum(m, jnp.max(s_past, axis=0, keepdims=True))
        p = jnp.exp(s_diag - m)
        den = jnp.sum(p, axis=0, keepdims=True)
        acc = _dot(vt_ref[dv, lo:hi], p.astype(BF16))
        if qi:
            p = jnp.exp(s_past - m)
            den = den + jnp.sum(p, axis=0, keepdims=True)
            acc = acc + _dot(vt_ref[dv, 0:lo], p.astype(BF16))
        o_ref[lo:hi, dv] = jnp.transpose(acc / den).astype(BF16)

    units = [(h, qi) for qi in reversed(range(seq // tq)) for h in range(MLA_HEADS_PER_STEP)]
    ahead = scores(*units[0])
    for i, unit in enumerate(units):
        current = ahead
        if i + 1 < len(units):
            ahead = scores(*units[i + 1])
        finish(*unit, *current)


def _mla_attn(q, k, vt, batch, seq):
    n = q.shape[0]
    hs = MLA_HEADS_PER_STEP
    blk = lambda w: pl.BlockSpec((seq, hs * w), lambda b, h: (b, h))
    return pl.pallas_call(
        _mla_attn_kernel,
        grid=(batch, MLA_HEADS // hs),
        in_specs=[blk(MLA_QK_PAD), blk(MLA_QK_PAD),
                  pl.BlockSpec((hs * MLA_V, seq), lambda b, h: (b * (MLA_HEADS // hs) + h, 0))],
        out_specs=blk(MLA_V),
        out_shape=jax.ShapeDtypeStruct((n, MLA_HEADS * MLA_V), BF16),
        compiler_params=_params("parallel", "parallel"),
        name="mla_attn",
    )(q, k, vt)


def _mla_front(x, pos, w_in, q_norm, kv_norm, w_uq, w_ukv, batch, seq):
    H, half = MLA_HEADS, MLA_ROPE // 2
    pad = LANES - MLA_ROPE

    def rope_cols(w, swap):
        a, b = w[..., :half], w[..., half:]
        if swap:
            a, b = b, a
        return jnp.concatenate([a, b, jnp.zeros(w.shape[:-1] + (pad,), w.dtype)], axis=-1)

    kr_w = w_in[:, MLA_Q_RANK + MLA_KV_RANK:]
    win = jnp.concatenate([w_in[:, :MLA_Q_RANK + MLA_KV_RANK], rope_cols(kr_w, False),
                           rope_cols(kr_w, True)], axis=1).astype(BF16)
    wq3 = w_uq.reshape(MLA_Q_RANK, H, MLA_NOPE + MLA_ROPE)
    wq = jnp.concatenate([wq3[:, :, :MLA_NOPE].reshape(MLA_Q_RANK, H * LANES),
                          rope_cols(wq3[:, :, MLA_NOPE:], False).reshape(MLA_Q_RANK, H * LANES),
                          rope_cols(wq3[:, :, MLA_NOPE:], True).reshape(MLA_Q_RANK, H * LANES)],
                         axis=1).astype(BF16)
    wkv3 = w_ukv.reshape(MLA_KV_RANK, H, MLA_NOPE + MLA_V)
    wk = wkv3[:, :, :MLA_NOPE].reshape(MLA_KV_RANK, H * MLA_NOPE).astype(BF16)
    wvt = wkv3[:, :, MLA_NOPE:].reshape(MLA_KV_RANK, H * MLA_V).T.astype(BF16)
    cos, sin = _rope_tables(pos)
    q, k, v = _mla_proj(x, cos, sin, win, q_norm[None, :], kv_norm[None, :], wq, wk, wvt, seq)
    return _mla_attn(q, k, v, batch, seq)


def _conv_kernel(x_ref, win_ref, cw_ref, o_ref, cu_ref, xb_ref):
    ts = x_ref.shape[0]
    halo = SUBLANES

    @pl.when(pl.program_id(1) == 0)
    def _():
        cu_ref[0:halo, :] = jnp.zeros((halo, D_MODEL), F32)

    cw = cw_ref[0]

    xb_ref[...] = x_ref[...].astype(BF16)

    def stages(c0):
        cols = slice(c0, c0 + MXU_WIDTH)
        xb = xb_ref[...]
        b = _dot(xb, win_ref[0, :, c0:c0 + MXU_WIDTH])
        cu = (_dot(xb, win_ref[0, :, D_MODEL + c0:D_MODEL + c0 + MXU_WIDTH])
              * _dot(xb, win_ref[0, :, 2 * D_MODEL + c0:2 * D_MODEL + c0 + MXU_WIDTH]))
        cu_ref[halo:halo + ts, cols] = cu
        yield
        z = (cw[0:1, cols] * cu_ref[halo - 2:halo - 2 + ts, cols]
             + cw[1:2, cols] * cu_ref[halo - 1:halo - 1 + ts, cols]
             + cw[2:3, cols] * cu)
        o_ref[:, cols] = (b * z).astype(BF16)
        cu_ref[0:halo, cols] = cu_ref[ts:ts + halo, cols]

    _skewed([stages(c0) for c0 in range(0, D_MODEL, MXU_WIDTH)])


def _conv_front(x, w_in, conv_w_all, layer, batch, seq):
    n = x.shape[0]
    ts = CONV_ROWS
    per_b = seq // ts
    rows = lambda w: pl.BlockSpec((ts, w), lambda b, j: (b * per_b + j, 0))
    return pl.pallas_call(
        _conv_kernel,
        grid=(batch, per_b),
        in_specs=[rows(D_MODEL), _layer_spec(*w_in), _layer_spec(conv_w_all, layer)],
        out_specs=rows(D_MODEL),
        out_shape=jax.ShapeDtypeStruct((n, D_MODEL), BF16),
        scratch_shapes=[pltpu.VMEM((SUBLANES + ts, D_MODEL), F32), pltpu.VMEM((ts, D_MODEL), BF16)],
        compiler_params=_params("parallel", "arbitrary"),
        name="conv_front",
    )(x, w_in[0], conv_w_all)


def kernel(x, p, positions, gla_w_in, gla_w_gate_up, gla_b_gate, gla_norm_g, gla_w_out,
           mla_w_in, mla_q_norm, mla_kv_norm, mla_w_uq, mla_w_ukv, mla_w_out,
           conv_w_in, conv_w, conv_w_out, ln_g, ln_b, mlp_w1, mlp_w2, ple_w_gate, ple_w_proj):
    batch, seq, d = x.shape
    n = batch * seq
    assert d == D_MODEL and seq % GLA_ROWS == 0 and seq % CONV_ROWS == 0 and seq % MLA_Q_TILE == 0
    xf = x.reshape(n, d)
    pos = positions.astype(F32).reshape(n)
    p_all = p.reshape(DEPTH * n, PLE_DIM)
    wp_all = ple_w_proj.astype(BF16)
    gla_small = _gla_small_weights(gla_w_in, gla_w_gate_up, gla_b_gate, gla_norm_g)
    w_out_f32 = (gla_w_out, mla_w_out, conv_w_out)
    mixer_w_in_f32 = (gla_w_in, None, conv_w_in)

    first = lambda stack: (stack[:1].astype(BF16), 0)
    wout, w1, w2, wg = first(w_out_f32[0]), first(mlp_w1), first(mlp_w2), first(ple_w_gate)
    mixer_w_in = first(gla_w_in)
    for i in range(DEPTH):
        j, kind = i // N_MIXERS, i % N_MIXERS
        if kind == 0:
            o = _gla_front(xf, mixer_w_in, gla_small, j, batch, seq)
        elif kind == 1:
            o = _mla_front(xf, pos, mla_w_in[j], mla_q_norm[j], mla_kv_norm[j], mla_w_uq[j],
                           mla_w_ukv[j], batch, seq)
        else:
            o = _conv_front(xf, mixer_w_in, conv_w, j, batch, seq)
        casts = []
        if i + 1 < DEPTH:
            j_next, kind_next = (i + 1) // N_MIXERS, (i + 1) % N_MIXERS
            casts = [(w_out_f32[kind_next], j_next), (mlp_w1, i + 1), (mlp_w2, i + 1), (ple_w_gate, i + 1)]
            if mixer_w_in_f32[kind_next] is not None:
                casts.append((mixer_w_in_f32[kind_next], j_next))
        xf, cast = _tail(o, xf, p_all, i, wout, w1, w2, wg, wp_all, ln_g, ln_b, casts)
        if cast:
            wout, w1, w2, wg = [(c, 0) for c in cast[:4]]
            mixer_w_in = (cast[4], 0) if len(cast) > 4 else None
    return xf.reshape(batch, seq, d)
```

```python
import functools

import numpy as np
import jax
import jax.numpy as jnp
from jax.experimental import pallas as pl
from jax.experimental.pallas import tpu as pltpu

F32 = jnp.float32
BF16 = jnp.bfloat16

D_MODEL = 1024
DEPTH = 4
CHUNK = 64
N_MIXERS = 3
ALPHA = (2 * DEPTH) ** 0.25
LN_EPS = 1e-5
RMS_EPS = 1e-6
PLE_DIM = 256
D_FF = 4 * D_MODEL

GLA_HEADS = 4
GLA_DK = 128
GLA_DV = 256
GLA_GATE_RANK = 16
GLA_TAU = 16.0
GLA_HK = GLA_HEADS * GLA_DK
GLA_HV = GLA_HEADS * GLA_DV
CHUNK_SHIFT = CHUNK.bit_length() - 1
assert 1 << CHUNK_SHIFT == CHUNK
GLA_LEVELS = CHUNK_SHIFT

MLA_HEADS = 8
MLA_NOPE = 128
MLA_ROPE = 64
MLA_V = 128
MLA_Q_RANK = 256
MLA_KV_RANK = 256
MLA_QK_PAD = 256
ROPE_BASE = 10000.0

CONV_WIDTH = 3

LANES = 128
SUBLANES = 8
MXU_WIDTH = 256
VMEM_LIMIT_BYTES = 56 * 1024 * 1024

TAIL_ROWS = 512
TAIL_SUB_ROWS = 256
FF_CHUNK = 1024
GLA_ROWS = 512
GLA_ITEMS_UP_FRONT = 1
TRI_ROWS = 256
MLA_PROJ_ROWS = 512
MLA_PROJ_SUB_ROWS = 256
ROPE_TABLE_ROWS = 512
MLA_Q_TILE = 256
MLA_HEADS_PER_STEP = 2
CONV_ROWS = 512


def _params(*semantics):
    return pltpu.CompilerParams(dimension_semantics=semantics, vmem_limit_bytes=VMEM_LIMIT_BYTES)


def _const_spec(shape):
    return pl.BlockSpec(shape, lambda *_: (0,) * len(shape), pipeline_mode=pl.Buffered(1))


def _dot(a, b):
    return jnp.dot(a, b, preferred_element_type=F32)


def _dot_nt(a, b):
    return jax.lax.dot_general(a, b, (((1,), (1,)), ((), ())), preferred_element_type=F32)


def _dot_tn(a, b):
    return jax.lax.dot_general(a, b, (((0,), (0,)), ((), ())), preferred_element_type=F32)


def _layer_norm(x, g, b):
    mu = jnp.mean(x, axis=-1, keepdims=True)
    xc = x - mu
    var = jnp.mean(xc * xc, axis=-1, keepdims=True)
    return xc * jax.lax.rsqrt(var + LN_EPS) * g + b


def _rms_norm(x, g):
    return x * jax.lax.rsqrt(jnp.mean(x * x, axis=-1, keepdims=True) + RMS_EPS) * g


def _sigmoid(x):
    return 1.0 / (1.0 + jnp.exp(-x))


def _skewed(generators):
    live = list(enumerate(generators))
    t = 0
    while live:
        still = []
        for i, gen in live:
            if t < i:
                still.append((i, gen))
                continue
            try:
                next(gen)
                still.append((i, gen))
            except StopIteration:
                pass
        live = still
        t += 1


def _tail_kernel(n_casts, o_ref, x_ref, p_ref, wout_ref, w1_ref, w2_ref, wg_ref, wp_ref, lng_ref, lnb_ref,
                 *rest):
    cast_src, out_ref, cast_dst = rest[:n_casts], rest[n_casts], rest[n_casts + 1:]
    _cast_slabs(cast_src, cast_dst)

    g0, g1 = lng_ref[0, 0:1, :], lng_ref[0, 1:2, :]
    b0, b1 = lnb_ref[0, 0:1, :], lnb_ref[0, 1:2, :]

    def stages(rows):
        h = _dot(o_ref[rows, :], wout_ref[0])
        yield
        x1 = _layer_norm(ALPHA * x_ref[rows, :] + h, g0, b0)
        x1b = x1.astype(BF16)
        yield
        acc = None
        for c in range(D_FF // FF_CHUNK):
            cols = slice(c * FF_CHUNK, (c + 1) * FF_CHUNK)
            hc = _dot(x1b, w1_ref[0, :, cols])
            hc = jnp.square(jnp.maximum(hc, 0.0)).astype(BF16)
            part = _dot(hc, w2_ref[0, cols, :])
            acc = part if acc is None else acc + part
            yield
        x2 = _layer_norm(ALPHA * x1 + acc, g1, b1)
        x2b = x2.astype(BF16)
        yield
        gate_in = _dot(x2b, wg_ref[0])
        proj = _dot(p_ref[rows, :].astype(BF16), wp_ref[0])
        yield
        out_ref[rows, :] = x2 + _sigmoid(gate_in) * proj

    _skewed([stages(slice(r, r + TAIL_SUB_ROWS)) for r in range(0, TAIL_ROWS, TAIL_SUB_ROWS)])


def _layer_spec(stack, layer):
    shape = (1,) + stack.shape[1:]
    return pl.BlockSpec(shape, lambda *_: (layer,) + (0,) * (len(shape) - 1), pipeline_mode=pl.Buffered(1))


def _cast_specs(casts, steps):
    in_specs, out_specs, shapes = [], [], []
    for stack, index in casts:
        _, r, c = stack.shape
        slab = r // steps
        assert slab * steps == r and slab % (2 * SUBLANES) == 0
        in_specs.append(pl.BlockSpec(
            (1, slab, c), functools.partial(lambda index, i: (index, jnp.minimum(i, steps - 1), 0), index)))
        out_specs.append(pl.BlockSpec((1, slab, c), lambda i: (0, jnp.minimum(i, steps - 1), 0)))
        shapes.append(jax.ShapeDtypeStruct((1, r, c), BF16))
    return in_specs, out_specs, shapes


def _cast_slabs(srcs, dsts):
    for src, dst in zip(srcs, dsts):
        dst[...] = src[...].astype(BF16)


def _tail(o, x, p_all, layer, wout, w1, w2, wg, wp_all, lng_all, lnb_all, casts):
    n = x.shape[0]
    tm = TAIL_ROWS
    steps = n // tm
    rows = lambda w: pl.BlockSpec((tm, w), lambda i: (i, 0))
    p_rows = pl.BlockSpec((tm, PLE_DIM), lambda i: (layer * steps + i, 0))
    cast_in, cast_out, cast_shapes = _cast_specs(casts, steps)
    outs = pl.pallas_call(
        functools.partial(_tail_kernel, len(casts)),
        grid=(steps,),
        in_specs=[rows(D_MODEL), rows(D_MODEL), p_rows,
                  _layer_spec(*wout), _layer_spec(*w1), _layer_spec(*w2), _layer_spec(*wg),
                  _layer_spec(wp_all, layer), _layer_spec(lng_all, layer), _layer_spec(lnb_all, layer)]
                 + cast_in,
        out_specs=[rows(D_MODEL)] + cast_out,
        out_shape=[jax.ShapeDtypeStruct((n, D_MODEL), F32)] + cast_shapes,
        compiler_params=_params("parallel"),
        name="tail",
    )(o, x, p_all, wout[0], w1[0], w2[0], wg[0], wp_all, lng_all, lnb_all, *[s for s, _ in casts])
    return outs[0], outs[1:]


def _split3_bf16(a):
    hi = a.astype(BF16)
    r1 = a - hi.astype(F32)
    mid = r1.astype(BF16)
    lo = (r1 - mid.astype(F32)).astype(BF16)
    return hi, mid, lo


def _gla_project_items(x_ref, xb_ref, wqkvr_ref, wlr_ref, wgu_ref, bg_ref, tri_ref,
                       q_ref, k_ref, v_ref, r_ref, l_ref):
    w = MXU_WIDTH

    def cast_x():
        xb_ref[...] = x_ref[...].astype(BF16)

    def column_tile(dst_ref, dst_col, w_col, finish):
        def item():
            y = _dot(xb_ref[...], wqkvr_ref[0, :, w_col:w_col + w])
            dst_ref[:, dst_col:dst_col + w] = finish(y)
        return item

    def log_decay():
        g_lr = _dot(xb_ref[...], wlr_ref[0])
        z = _dot(g_lr.astype(BF16), wgu_ref[0]) + bg_ref[0]
        log_a = (jnp.minimum(z, 0.0) - jnp.log1p(jnp.exp(-jnp.abs(z)))) * (1.0 / GLA_TAU)
        tri = tri_ref[...]
        for r0 in range(0, GLA_ROWS, TRI_ROWS):
            hi, mid, lo = _split3_bf16(log_a[r0:r0 + TRI_ROWS, :])
            l_ref[r0:r0 + TRI_ROWS, :] = (_dot(tri, lo) + _dot(tri, mid)) + _dot(tri, hi)

    items = [cast_x, log_decay]
    groups = [(q_ref, GLA_HK, lambda y: y * (GLA_DK ** -0.5)), (k_ref, GLA_HK, lambda y: y),
              (v_ref, GLA_HV, lambda y: y.astype(BF16)), (r_ref, GLA_HV, lambda y: y * _sigmoid(y))]
    w_col = 0
    for dst_ref, width, finish in groups:
        for dst_col in range(0, width, w):
            items.append(column_tile(dst_ref, dst_col, w_col, finish))
            w_col += w
    return items


def _level_ref_rows(L, level):
    C, dk = L.shape
    m = 1 << level
    if m >= SUBLANES:
        parts = [jnp.broadcast_to(L[b + m - 1:b + m, :], (2 * m, dk)) for b in range(0, C, 2 * m)]
        return parts[0] if len(parts) == 1 else jnp.concatenate(parts, axis=0)
    if m == 1:
        row = jax.lax.broadcasted_iota(jnp.int32, (C, dk), 0)
        return jnp.where((row & 1) == 1, pltpu.roll(L, shift=1, axis=0), L)
    tiles = (C // SUBLANES, SUBLANES, dk)
    L3 = L.reshape(tiles)
    pick = lambda i: jnp.broadcast_to(L3[:, i:i + 1, :], tiles)
    if m == 4:
        ref = pick(3)
    else:
        sub = jax.lax.broadcasted_iota(jnp.int32, tiles, 1)
        ref = jnp.where(sub < 4, pick(1), pick(5))
    return ref.reshape(C, dk)


def _gla_chunk(qs, ks, vs, Ls, states_t, pair_level, tick):
    C = CHUNK
    heads = range(len(qs))
    l_ends = [L[C - 1:C, :] for L in Ls]
    os = [_dot_nt((qs[h] * jnp.exp(Ls[h])).astype(BF16), states_t[h].astype(BF16)) for h in heads]
    k_decs = [(ks[h] * jnp.exp(l_ends[h] - Ls[h])).astype(BF16) for h in heads]
    new_states_t = [states_t[h] * jnp.exp(l_ends[h]) + _dot_tn(vs[h], k_decs[h]) for h in heads]
    tick()

    scores = [jnp.where(pair_level < 0, jnp.sum(qs[h] * ks[h], axis=1, keepdims=True), 0.0)
              for h in heads]
    for level in range(GLA_LEVELS):
        for h in heads:
            f = jnp.exp(-jnp.abs(Ls[h] - _level_ref_rows(Ls[h], level)))
            s_level = _dot_nt((qs[h] * f).astype(BF16), (ks[h] * f).astype(BF16))
            scores[h] = jnp.where(pair_level == level, s_level, scores[h])
        tick()

    os = [os[h] + _dot(scores[h].astype(BF16), vs[h]) for h in heads]
    return os, new_states_t


def _gla_scan_tile(q_ref, k_ref, v_ref, r_ref, l_ref, g_ref, lvl_ref, o_ref, state_ref, tick):
    g = g_ref[0]
    pair_level = lvl_ref[...]

    for c in range(GLA_ROWS // CHUNK):
        rows = slice(c * CHUNK, (c + 1) * CHUNK)
        kcs = [slice(h * GLA_DK, (h + 1) * GLA_DK) for h in range(GLA_HEADS)]
        vcs = [slice(h * GLA_DV, (h + 1) * GLA_DV) for h in range(GLA_HEADS)]
        os, new_states = _gla_chunk([q_ref[rows, kc] for kc in kcs], [k_ref[rows, kc] for kc in kcs],
                                    [v_ref[rows, vc] for vc in vcs], [l_ref[rows, kc] for kc in kcs],
                                    [state_ref[h] for h in range(GLA_HEADS)], pair_level, tick)
        for h in range(GLA_HEADS):
            state_ref[h] = new_states[h]
            o_ref[rows, vcs[h]] = (_rms_norm(os[h], g) * r_ref[rows, vcs[h]]).astype(BF16)
        tick()


def _gla_front_kernel(tiles_per_seq, n_casts, x_ref, wqkvr_ref, wlr_ref, wgu_ref, bg_ref, tri_ref, g_ref,
                      lvl_ref, *rest):
    cast_src, o_ref, cast_dst = rest[:n_casts], rest[n_casts], rest[n_casts + 1:2 * n_casts + 1]
    state_ref, xb_ref = rest[2 * n_casts + 1:2 * n_casts + 3]
    slots = rest[2 * n_casts + 3:]
    step = pl.program_id(0)
    slot_a, slot_b = slots[:5], slots[5:]
    _cast_slabs(cast_src, cast_dst)

    @pl.when(step == 0)
    def _():
        for ref in slot_b:
            ref[...] = jnp.zeros_like(ref)

    @pl.when((step == 0) | (jax.lax.rem(step - 1, tiles_per_seq) == 0))
    def _():
        state_ref[...] = jnp.zeros_like(state_ref)

    def body(dst, src):
        items = _gla_project_items(x_ref, xb_ref, wqkvr_ref, wlr_ref, wgu_ref, bg_ref, tri_ref, *dst)
        n_items, n_ticks = len(items), (GLA_ROWS // CHUNK) * (GLA_LEVELS + 2)
        done = [0, 0]

        def tick():
            done[0] += 1
            while (done[1] < n_items and (done[1] - GLA_ITEMS_UP_FRONT) * (n_ticks - 2)
                   < done[0] * (n_items - GLA_ITEMS_UP_FRONT)):
                items[done[1]]()
                done[1] += 1

        for item in items[:GLA_ITEMS_UP_FRONT]:
            item()
        done[1] = GLA_ITEMS_UP_FRONT
        _gla_scan_tile(*src, g_ref, lvl_ref, o_ref, state_ref, tick)
        assert done == [n_ticks, n_items]

    parity = jax.lax.rem(step, 2)
    pl.when(parity == 0)(lambda: body(slot_a, slot_b))
    pl.when(parity == 1)(lambda: body(slot_b, slot_a))


def _gla_small_weights(w_in, w_gate_up, b_gate, norm_g):
    main = 2 * GLA_HK + GLA_HV + D_MODEL
    wlr = jnp.pad(w_in[:, :, main:], ((0, 0), (0, 0), (0, LANES - GLA_GATE_RANK))).astype(BF16)
    wgu = jnp.pad(w_gate_up, ((0, 0), (0, LANES - GLA_GATE_RANK), (0, 0))).astype(BF16)
    return wlr, wgu, b_gate[:, None, :], norm_g[:, None, :]


def _gla_front(x, w_in, small_weights, layer, batch, seq, casts):
    n = x.shape[0]
    t = GLA_ROWS
    n_tiles = n // t
    wlr, wgu, bg, g = small_weights
    idx = np.arange(TRI_ROWS)
    tri = ((idx[:, None] // CHUNK == idx[None, :] // CHUNK) & (idx[None, :] <= idx[:, None]))
    tri = jnp.asarray(tri, BF16)
    idx = np.arange(CHUNK)
    differ = idx[:, None] ^ idx[None, :]
    pair_level = jnp.asarray(np.floor(np.log2(np.maximum(differ, 1))).astype(np.int32) - (differ == 0))
    slot = [pltpu.VMEM((t, GLA_HK), F32), pltpu.VMEM((t, GLA_HK), F32), pltpu.VMEM((t, GLA_HV), BF16),
            pltpu.VMEM((t, GLA_HV), F32), pltpu.VMEM((t, GLA_HK), F32)]
    cast_in, cast_out, cast_shapes = _cast_specs(casts, n_tiles)
    outs = pl.pallas_call(
        functools.partial(_gla_front_kernel, seq // t, len(casts)),
        grid=(n_tiles + 1,),
        in_specs=[pl.BlockSpec((t, D_MODEL), lambda s: (jnp.minimum(s, n_tiles - 1), 0)),
                  _layer_spec(*w_in), _layer_spec(wlr, layer), _layer_spec(wgu, layer),
                  _layer_spec(bg, layer), _const_spec(tri.shape), _layer_spec(g, layer),
                  _const_spec(pair_level.shape)] + cast_in,
        out_specs=[pl.BlockSpec((t, GLA_HV), lambda s: (jnp.maximum(s - 1, 0), 0))] + cast_out,
        out_shape=[jax.ShapeDtypeStruct((n, GLA_HV), BF16)] + cast_shapes,
        scratch_shapes=[pltpu.VMEM((GLA_HEADS, GLA_DV, GLA_DK), F32), pltpu.VMEM((t, D_MODEL), BF16)]
                       + slot + slot,
        compiler_params=_params("arbitrary"),
        name="gla_front",
    )(x, w_in[0], wlr, wgu, bg, tri, g, pair_level, *[s for s, _ in casts])
    return outs[0], outs[1:]


def _rope_table_kernel(pos_ref, freq_ref, cos_ref, sin_ref):
    ang = pos_ref[...] * freq_ref[...]
    cos_ref[...] = jnp.cos(ang)
    sin_ref[...] = jnp.sin(ang)


def _rope_tables(pos):
    n, half = pos.shape[0], MLA_ROPE // 2
    per_row = LANES // half
    inv_freq = ROPE_BASE ** (-jnp.arange(0, half, dtype=F32) * (2.0 / MLA_ROPE))
    pos_t = pos.reshape(n // MLA_PROJ_ROWS, per_row, MLA_PROJ_ROWS // per_row).transpose(0, 2, 1)
    pos_d = jnp.repeat(pos_t.reshape(n // per_row, per_row), half, axis=1)
    freq_d = jnp.tile(inv_freq, per_row)[None, :]
    rows_d = n // per_row
    tm = min(rows_d, ROPE_TABLE_ROWS)
    blk = pl.BlockSpec((tm, LANES), lambda i: (i, 0))
    cos_d, sin_d = pl.pallas_call(
        _rope_table_kernel,
        grid=(rows_d // tm,),
        in_specs=[blk, _const_spec(freq_d.shape)],
        out_specs=[blk, blk],
        out_shape=[jax.ShapeDtypeStruct((rows_d, LANES), F32)] * 2,
        compiler_params=_params("parallel"),
        name="rope_tables",
    )(pos_d, freq_d)
    return cos_d, sin_d


def _mla_proj_kernel(x_ref, cos_ref, sin_ref, win_ref, qn_ref, kvn_ref,
                     wq_ref, wk_ref, wvt_ref, q_ref, k_ref, vt_ref):
    H, R = MLA_HEADS, MLA_Q_RANK
    half = MLA_ROPE // 2
    group_rows = MLA_PROJ_ROWS // (LANES // half)
    scale = (MLA_NOPE + MLA_ROPE) ** -0.5

    def stages(r0):
        r1 = r0 + MLA_PROJ_SUB_ROWS
        c = _dot(x_ref[r0:r1, :].astype(BF16), win_ref[...])
        yield
        cq = _rms_norm(c[:, 0:R], qn_ref[...]).astype(BF16)
        ckv = _rms_norm(c[:, R:R + MLA_KV_RANK], kvn_ref[...]).astype(BF16)
        pad = jnp.zeros((group_rows, LANES - MLA_ROPE), F32)
        cos_blocks, sin_blocks = [], []
        for j in range(r0 // group_rows, r1 // group_rows):
            c32 = cos_ref[:, j * half:(j + 1) * half]
            s32 = sin_ref[:, j * half:(j + 1) * half]
            cos_blocks.append(jnp.concatenate([c32, c32, pad], axis=1))
            sin_blocks.append(jnp.concatenate([-s32, s32, pad], axis=1))
        cos = jnp.concatenate(cos_blocks, axis=0)
        sin = jnp.concatenate(sin_blocks, axis=0)
        k_rope = (c[:, 2 * R:2 * R + LANES] * cos + c[:, 2 * R + LANES:2 * R + 2 * LANES] * sin).astype(BF16)
        yield
        qq = _dot(cq, wq_ref[...])
        k_nope = _dot(ckv, wk_ref[...])
        vt_ref[:, r0:r1] = _dot_nt(wvt_ref[...], ckv).astype(BF16)
        yield
        for h in range(H):
            lo = h * LANES
            q_nope = qq[:, lo:lo + LANES]
            q_r = qq[:, H * LANES + lo:H * LANES + lo + LANES]
            q_sw = qq[:, 2 * H * LANES + lo:2 * H * LANES + lo + LANES]
            base = h * MLA_QK_PAD
            q_ref[r0:r1, base:base + LANES] = (q_nope * scale).astype(BF16)
            q_ref[r0:r1, base + LANES:base + 2 * LANES] = ((q_r * cos + q_sw * sin) * scale).astype(BF16)
            k_ref[r0:r1, base:base + LANES] = k_nope[:, lo:lo + LANES].astype(BF16)
            k_ref[r0:r1, base + LANES:base + 2 * LANES] = k_rope

    _skewed([stages(r0) for r0 in range(0, MLA_PROJ_ROWS, MLA_PROJ_SUB_ROWS)])


def _mla_proj(x, cos, sin, win, qn, kvn, wq, wk, wvt, seq):
    n = x.shape[0]
    tm = MLA_PROJ_ROWS
    per_b = seq // tm
    rows = lambda w: pl.BlockSpec((tm, w), lambda i: (i, 0))
    qk_w = MLA_HEADS * MLA_QK_PAD
    hv = MLA_HEADS * MLA_V
    table = pl.BlockSpec((tm // (LANES // (MLA_ROPE // 2)), LANES), lambda i: (i, 0))
    return pl.pallas_call(
        _mla_proj_kernel,
        grid=(n // tm,),
        in_specs=[rows(D_MODEL), table, table, _const_spec(win.shape), _const_spec(qn.shape), _const_spec(kvn.shape),
                  _const_spec(wq.shape), _const_spec(wk.shape), _const_spec(wvt.shape)],
        out_specs=[rows(qk_w), rows(qk_w),
                   pl.BlockSpec((hv, tm), lambda i: (i // per_b, i % per_b))],
        out_shape=[jax.ShapeDtypeStruct((n, qk_w), BF16), jax.ShapeDtypeStruct((n, qk_w), BF16),
                   jax.ShapeDtypeStruct((n // seq * hv, seq), BF16)],
        compiler_params=_params("parallel"),
        name="mla_proj",
    )(x, cos, sin, win, qn, kvn, wq, wk, wvt)


def _mla_attn_kernel(q_ref, k_ref, vt_ref, o_ref):
    seq = q_ref.shape[0]
    tq = MLA_Q_TILE
    key = jax.lax.broadcasted_iota(jnp.int32, (tq, tq), 0)
    qry = jax.lax.broadcasted_iota(jnp.int32, (tq, tq), 1)
    visible = jnp.right_shift(key, CHUNK_SHIFT) <= jnp.right_shift(qry, CHUNK_SHIFT)

    def scores(h, qi):
        lo, hi = qi * tq, (qi + 1) * tq
        qk = slice(h * MLA_QK_PAD, (h + 1) * MLA_QK_PAD)
        q = q_ref[lo:hi, qk]
        s_diag = jnp.where(visible, _dot_nt(k_ref[lo:hi, qk], q), -jnp.inf)
        s_past = _dot_nt(k_ref[0:lo, qk], q) if qi else None
        return s_diag, s_past

    def finish(h, qi, s_diag, s_past):
        lo, hi = qi * tq, (qi + 1) * tq
        dv = slice(h * MLA_V, (h + 1) * MLA_V)
        m = jnp.max(s_diag, axis=0, keepdims=True)
        if qi:
            m = jnp.maximum(m, jnp.max(s_past, axis=0, keepdims=True))
        p = jnp.exp(s_diag - m)
        den = jnp.sum(p, axis=0, keepdims=True)
        acc = _dot(vt_ref[dv, lo:hi], p.astype(BF16))
        if qi:
            p = jnp.exp(s_past - m)
            den = den + jnp.sum(p, axis=0, keepdims=True)
            acc = acc + _dot(vt_ref[dv, 0:lo], p.astype(BF16))
        o_ref[lo:hi, dv] = jnp.transpose(acc / den).astype(BF16)

    units = [(h, qi) for qi in reversed(range(seq // tq)) for h in range(MLA_HEADS_PER_STEP)]
    ahead = scores(*units[0])
    for i, unit in enumerate(units):
        current = ahead
        if i + 1 < len(units):
            ahead = scores(*units[i + 1])
        finish(*unit, *current)


def _mla_attn(q, k, vt, batch, seq):
    n = q.shape[0]
    hs = MLA_HEADS_PER_STEP
    blk = lambda w: pl.BlockSpec((seq, hs * w), lambda b, h: (b, h))
    return pl.pallas_call(
        _mla_attn_kernel,
        grid=(batch, MLA_HEADS // hs),
        in_specs=[blk(MLA_QK_PAD), blk(MLA_QK_PAD),
                  pl.BlockSpec((hs * MLA_V, seq), lambda b, h: (b * (MLA_HEADS // hs) + h, 0))],
        out_specs=blk(MLA_V),
        out_shape=jax.ShapeDtypeStruct((n, MLA_HEADS * MLA_V), BF16),
        compiler_params=_params("parallel", "parallel"),
        name="mla_attn",
    )(q, k, vt)


def _mla_front(x, pos, w_in, q_norm, kv_norm, w_uq, w_ukv, batch, seq):
    H, half = MLA_HEADS, MLA_ROPE // 2
    pad = LANES - MLA_ROPE

    def rope_cols(w, swap):
        a, b = w[..., :half], w[..., half:]
        if swap:
            a, b = b, a
        return jnp.concatenate([a, b, jnp.zeros(w.shape[:-1] + (pad,), w.dtype)], axis=-1)

    kr_w = w_in[:, MLA_Q_RANK + MLA_KV_RANK:]
    win = jnp.concatenate([w_in[:, :MLA_Q_RANK + MLA_KV_RANK], rope_cols(kr_w, False),
                           rope_cols(kr_w, True)], axis=1).astype(BF16)
    wq3 = w_uq.reshape(MLA_Q_RANK, H, MLA_NOPE + MLA_ROPE)
    wq = jnp.concatenate([wq3[:, :, :MLA_NOPE].reshape(MLA_Q_RANK, H * LANES),
                          rope_cols(wq3[:, :, MLA_NOPE:], False).reshape(MLA_Q_RANK, H * LANES),
                          rope_cols(wq3[:, :, MLA_NOPE:], True).reshape(MLA_Q_RANK, H * LANES)],
                         axis=1).astype(BF16)
    wkv3 = w_ukv.reshape(MLA_KV_RANK, H, MLA_NOPE + MLA_V)
    wk = wkv3[:, :, :MLA_NOPE].reshape(MLA_KV_RANK, H * MLA_NOPE).astype(BF16)
    wvt = wkv3[:, :, MLA_NOPE:].reshape(MLA_KV_RANK, H * MLA_V).T.astype(BF16)
    cos, sin = _rope_tables(pos)
    q, k, v = _mla_proj(x, cos, sin, win, q_norm[None, :], kv_norm[None, :], wq, wk, wvt, seq)
    return _mla_attn(q, k, v, batch, seq)


def _conv_kernel(x_ref, win_ref, cw_ref, o_ref, cu_ref, xb_ref):
    ts = x_ref.shape[0]
    halo = SUBLANES

    @pl.when(pl.program_id(1) == 0)
    def _():
        cu_ref[0:halo, :] = jnp.zeros((halo, D_MODEL), F32)

    cw = cw_ref[0]

    xb_ref[...] = x_ref[...].astype(BF16)

    def stages(c0):
        cols = slice(c0, c0 + MXU_WIDTH)
        xb = xb_ref[...]
        b = _dot(xb, win_ref[0, :, c0:c0 + MXU_WIDTH])
        cu = (_dot(xb, win_ref[0, :, D_MODEL + c0:D_MODEL + c0 + MXU_WIDTH])
              * _dot(xb, win_ref[0, :, 2 * D_MODEL + c0:2 * D_MODEL + c0 + MXU_WIDTH]))
        cu_ref[halo:halo + ts, cols] = cu
        yield
        z = (cw[0:1, cols] * cu_ref[halo - 2:halo - 2 + ts, cols]
             + cw[1:2, cols] * cu_ref[halo - 1:halo - 1 + ts, cols]
             + cw[2:3, cols] * cu)
        o_ref[:, cols] = (b * z).astype(BF16)
        cu_ref[0:halo, cols] = cu_ref[ts:ts + halo, cols]

    _skewed([stages(c0) for c0 in range(0, D_MODEL, MXU_WIDTH)])


def _conv_front(x, w_in, conv_w_all, layer, batch, seq):
    n = x.shape[0]
    ts = CONV_ROWS
    per_b = seq // ts
    rows = lambda w: pl.BlockSpec((ts, w), lambda b, j: (b * per_b + j, 0))
    return pl.pallas_call(
        _conv_kernel,
        grid=(batch, per_b),
        in_specs=[rows(D_MODEL), _layer_spec(*w_in), _layer_spec(conv_w_all, layer)],
        out_specs=rows(D_MODEL),
        out_shape=jax.ShapeDtypeStruct((n, D_MODEL), BF16),
        scratch_shapes=[pltpu.VMEM((SUBLANES + ts, D_MODEL), F32), pltpu.VMEM((ts, D_MODEL), BF16)],
        compiler_params=_params("parallel", "arbitrary"),
        name="conv_front",
    )(x, w_in[0], conv_w_all)


def kernel(x, p, positions, gla_w_in, gla_w_gate_up, gla_b_gate, gla_norm_g, gla_w_out,
           mla_w_in, mla_q_norm, mla_kv_norm, mla_w_uq, mla_w_ukv, mla_w_out,
           conv_w_in, conv_w, conv_w_out, ln_g, ln_b, mlp_w1, mlp_w2, ple_w_gate, ple_w_proj):
    batch, seq, d = x.shape
    n = batch * seq
    assert d == D_MODEL and seq % GLA_ROWS == 0 and seq % CONV_ROWS == 0 and seq % MLA_Q_TILE == 0
    xf = x.reshape(n, d)
    pos = positions.astype(F32).reshape(n)
    p_all = p.reshape(DEPTH * n, PLE_DIM)
    wp_all = ple_w_proj.astype(BF16)
    gla_small = _gla_small_weights(gla_w_in, gla_w_gate_up, gla_b_gate, gla_norm_g)
    gla_main = gla_w_in[:, :, :2 * GLA_HK + GLA_HV + D_MODEL].astype(BF16)
    w_out_f32 = (gla_w_out, mla_w_out, conv_w_out)
    mixer_w_in_f32 = (None, None, conv_w_in)

    tail_weights = lambda layer, kind, j: [(w_out_f32[kind], j), (mlp_w1, layer), (mlp_w2, layer),
                                           (ple_w_gate, layer)]
    mixer_w_in = None
    for i in range(DEPTH):
        j, kind = i // N_MIXERS, i % N_MIXERS
        if kind == 0:
            o, cast = _gla_front(xf, (gla_main, j), gla_small, j, batch, seq,
                                 tail_weights(0, kind, j) if i == 0 else [])
            if i == 0:
                wout, w1, w2, wg = [(c, 0) for c in cast]
        elif kind == 1:
            o = _mla_front(xf, pos, mla_w_in[j], mla_q_norm[j], mla_kv_norm[j], mla_w_uq[j],
                           mla_w_ukv[j], batch, seq)
        else:
            o = _conv_front(xf, mixer_w_in, conv_w, j, batch, seq)
        casts = []
        if i + 1 < DEPTH:
            j_next, kind_next = (i + 1) // N_MIXERS, (i + 1) % N_MIXERS
            casts = tail_weights(i + 1, kind_next, j_next)
            if mixer_w_in_f32[kind_next] is not None:
                casts.append((mixer_w_in_f32[kind_next], j_next))
        xf, cast = _tail(o, xf, p_all, i, wout, w1, w2, wg, wp_all, ln_g, ln_b, casts)
        if cast:
            wout, w1, w2, wg = [(c, 0) for c in cast[:4]]
            mixer_w_in = (cast[4], 0) if len(cast) > 4 else None
    return xf.reshape(batch, seq, d)
```

```python
import functools

import numpy as np
import jax
import jax.numpy as jnp
from jax.experimental import pallas as pl
from jax.experimental.pallas import tpu as pltpu

F32 = jnp.float32
BF16 = jnp.bfloat16

D_MODEL = 1024
DEPTH = 4
CHUNK = 64
N_MIXERS = 3
ALPHA = (2 * DEPTH) ** 0.25
LN_EPS = 1e-5
RMS_EPS = 1e-6
PLE_DIM = 256
D_FF = 4 * D_MODEL

GLA_HEADS = 4
GLA_DK = 128
GLA_DV = 256
GLA_GATE_RANK = 16
GLA_TAU = 16.0
GLA_HK = GLA_HEADS * GLA_DK
GLA_HV = GLA_HEADS * GLA_DV
CHUNK_SHIFT = CHUNK.bit_length() - 1
assert 1 << CHUNK_SHIFT == CHUNK
GLA_LEVELS = CHUNK_SHIFT

MLA_HEADS = 8
MLA_NOPE = 128
MLA_ROPE = 64
MLA_V = 128
MLA_Q_RANK = 256
MLA_KV_RANK = 256
MLA_QK_PAD = 256
ROPE_BASE = 10000.0

CONV_WIDTH = 3

LANES = 128
SUBLANES = 8
MXU_WIDTH = 256
VMEM_LIMIT_BYTES = 56 * 1024 * 1024

TAIL_ROWS = 512
TAIL_SUB_ROWS = 256
FF_CHUNK = 1024
GLA_ROWS = 512
GLA_ITEMS_UP_FRONT = 1
GLA_W_IN_ROWS = 256
TRI_ROWS = 256
MLA_PROJ_ROWS = 512
MLA_PROJ_SUB_ROWS = 256
ROPE_TABLE_ROWS = 512
MLA_Q_TILE = 256
MLA_HEADS_PER_STEP = 2
CONV_ROWS = 512


def _params(*semantics):
    return pltpu.CompilerParams(dimension_semantics=semantics, vmem_limit_bytes=VMEM_LIMIT_BYTES)


def _const_spec(shape):
    return pl.BlockSpec(shape, lambda *_: (0,) * len(shape), pipeline_mode=pl.Buffered(1))


def _dot(a, b):
    return jnp.dot(a, b, preferred_element_type=F32)


def _dot_nt(a, b):
    return jax.lax.dot_general(a, b, (((1,), (1,)), ((), ())), preferred_element_type=F32)


def _dot_tn(a, b):
    return jax.lax.dot_general(a, b, (((0,), (0,)), ((), ())), preferred_element_type=F32)


def _layer_norm(x, g, b):
    mu = jnp.mean(x, axis=-1, keepdims=True)
    xc = x - mu
    var = jnp.mean(xc * xc, axis=-1, keepdims=True)
    return xc * jax.lax.rsqrt(var + LN_EPS) * g + b


def _rms_norm(x, g):
    return x * jax.lax.rsqrt(jnp.mean(x * x, axis=-1, keepdims=True) + RMS_EPS) * g


def _sigmoid(x):
    return 1.0 / (1.0 + jnp.exp(-x))


def _skewed(generators):
    live = list(enumerate(generators))
    t = 0
    while live:
        still = []
        for i, gen in live:
            if t < i:
                still.append((i, gen))
                continue
            try:
                next(gen)
                still.append((i, gen))
            except StopIteration:
                pass
        live = still
        t += 1


def _tail_kernel(n_casts, o_ref, x_ref, p_ref, wout_ref, w1_ref, w2_ref, wg_ref, wp_ref, lng_ref, lnb_ref,
                 *rest):
    cast_src, out_ref, cast_dst = rest[:n_casts], rest[n_casts], rest[n_casts + 1:]
    _cast_slabs(cast_src, cast_dst)

    g0, g1 = lng_ref[0, 0:1, :], lng_ref[0, 1:2, :]
    b0, b1 = lnb_ref[0, 0:1, :], lnb_ref[0, 1:2, :]

    def stages(rows):
        h = _dot(o_ref[rows, :], wout_ref[0])
        yield
        x1 = _layer_norm(ALPHA * x_ref[rows, :] + h, g0, b0)
        x1b = x1.astype(BF16)
        yield
        acc = None
        for c in range(D_FF // FF_CHUNK):
            cols = slice(c * FF_CHUNK, (c + 1) * FF_CHUNK)
            hc = _dot(x1b, w1_ref[0, :, cols])
            hc = jnp.square(jnp.maximum(hc, 0.0)).astype(BF16)
            part = _dot(hc, w2_ref[0, cols, :])
            acc = part if acc is None else acc + part
            yield
        x2 = _layer_norm(ALPHA * x1 + acc, g1, b1)
        x2b = x2.astype(BF16)
        yield
        gate_in = _dot(x2b, wg_ref[0])
        proj = _dot(p_ref[rows, :].astype(BF16), wp_ref[0])
        yield
        out_ref[rows, :] = x2 + _sigmoid(gate_in) * proj

    _skewed([stages(slice(r, r + TAIL_SUB_ROWS)) for r in range(0, TAIL_ROWS, TAIL_SUB_ROWS)])


def _layer_spec(stack, layer):
    shape = (1,) + stack.shape[1:]
    return pl.BlockSpec(shape, lambda *_: (layer,) + (0,) * (len(shape) - 1), pipeline_mode=pl.Buffered(1))


def _cast_specs(casts, steps):
    in_specs, out_specs, shapes = [], [], []
    for stack, index in casts:
        _, r, c = stack.shape
        slab = r // steps
        assert slab * steps == r and slab % (2 * SUBLANES) == 0
        in_specs.append(pl.BlockSpec(
            (1, slab, c), functools.partial(lambda index, i: (index, jnp.minimum(i, steps - 1), 0), index)))
        out_specs.append(pl.BlockSpec((1, slab, c), lambda i: (0, jnp.minimum(i, steps - 1), 0)))
        shapes.append(jax.ShapeDtypeStruct((1, r, c), BF16))
    return in_specs, out_specs, shapes


def _cast_slabs(srcs, dsts):
    for src, dst in zip(srcs, dsts):
        dst[...] = src[...].astype(BF16)


def _tail(o, x, p_all, layer, wout, w1, w2, wg, wp_all, lng_all, lnb_all, casts):
    n = x.shape[0]
    tm = TAIL_ROWS
    steps = n // tm
    rows = lambda w: pl.BlockSpec((tm, w), lambda i: (i, 0))
    p_rows = pl.BlockSpec((tm, PLE_DIM), lambda i: (layer * steps + i, 0))
    cast_in, cast_out, cast_shapes = _cast_specs(casts, steps)
    outs = pl.pallas_call(
        functools.partial(_tail_kernel, len(casts)),
        grid=(steps,),
        in_specs=[rows(D_MODEL), rows(D_MODEL), p_rows,
                  _layer_spec(*wout), _layer_spec(*w1), _layer_spec(*w2), _layer_spec(*wg),
                  _layer_spec(wp_all, layer), _layer_spec(lng_all, layer), _layer_spec(lnb_all, layer)]
                 + cast_in,
        out_specs=[rows(D_MODEL)] + cast_out,
        out_shape=[jax.ShapeDtypeStruct((n, D_MODEL), F32)] + cast_shapes,
        compiler_params=_params("parallel"),
        name="tail",
    )(o, x, p_all, wout[0], w1[0], w2[0], wg[0], wp_all, lng_all, lnb_all, *[s for s, _ in casts])
    return outs[0], outs[1:]


def _split3_bf16(a):
    hi = a.astype(BF16)
    r1 = a - hi.astype(F32)
    mid = r1.astype(BF16)
    lo = (r1 - mid.astype(F32)).astype(BF16)
    return hi, mid, lo


def _gla_project_items(x_ref, xb_ref, wqkvr_ref, wlr_ref, wgu_ref, bg_ref, tri_ref,
                       q_ref, k_ref, v_ref, r_ref, l_ref):
    w = MXU_WIDTH

    def cast_x():
        xb_ref[...] = x_ref[...].astype(BF16)

    def column_tile(dst_ref, dst_col, w_col, finish):
        def item():
            y = _dot(xb_ref[...], wqkvr_ref[0, :, w_col:w_col + w])
            dst_ref[:, dst_col:dst_col + w] = finish(y)
        return item

    def log_decay():
        g_lr = _dot(xb_ref[...], wlr_ref[0])
        z = _dot(g_lr.astype(BF16), wgu_ref[0]) + bg_ref[0]
        log_a = (jnp.minimum(z, 0.0) - jnp.log1p(jnp.exp(-jnp.abs(z)))) * (1.0 / GLA_TAU)
        tri = tri_ref[...]
        for r0 in range(0, GLA_ROWS, TRI_ROWS):
            hi, mid, lo = _split3_bf16(log_a[r0:r0 + TRI_ROWS, :])
            l_ref[r0:r0 + TRI_ROWS, :] = (_dot(tri, lo) + _dot(tri, mid)) + _dot(tri, hi)

    items = [cast_x, log_decay]
    groups = [(q_ref, GLA_HK, lambda y: y * (GLA_DK ** -0.5)), (k_ref, GLA_HK, lambda y: y),
              (v_ref, GLA_HV, lambda y: y.astype(BF16)), (r_ref, GLA_HV, lambda y: y * _sigmoid(y))]
    w_col = 0
    for dst_ref, width, finish in groups:
        for dst_col in range(0, width, w):
            items.append(column_tile(dst_ref, dst_col, w_col, finish))
            w_col += w
    return items


def _level_ref_rows(L, level):
    C, dk = L.shape
    m = 1 << level
    if m >= SUBLANES:
        parts = [jnp.broadcast_to(L[b + m - 1:b + m, :], (2 * m, dk)) for b in range(0, C, 2 * m)]
        return parts[0] if len(parts) == 1 else jnp.concatenate(parts, axis=0)
    if m == 1:
        row = jax.lax.broadcasted_iota(jnp.int32, (C, dk), 0)
        return jnp.where((row & 1) == 1, pltpu.roll(L, shift=1, axis=0), L)
    tiles = (C // SUBLANES, SUBLANES, dk)
    L3 = L.reshape(tiles)
    pick = lambda i: jnp.broadcast_to(L3[:, i:i + 1, :], tiles)
    if m == 4:
        ref = pick(3)
    else:
        sub = jax.lax.broadcasted_iota(jnp.int32, tiles, 1)
        ref = jnp.where(sub < 4, pick(1), pick(5))
    return ref.reshape(C, dk)


def _gla_chunk(qs, ks, vs, Ls, states_t, pair_level, tick):
    C = CHUNK
    heads = range(len(qs))
    l_ends = [L[C - 1:C, :] for L in Ls]
    os = [_dot_nt((qs[h] * jnp.exp(Ls[h])).astype(BF16), states_t[h].astype(BF16)) for h in heads]
    k_decs = [(ks[h] * jnp.exp(l_ends[h] - Ls[h])).astype(BF16) for h in heads]
    new_states_t = [states_t[h] * jnp.exp(l_ends[h]) + _dot_tn(vs[h], k_decs[h]) for h in heads]
    tick()

    scores = [jnp.where(pair_level < 0, jnp.sum(qs[h] * ks[h], axis=1, keepdims=True), 0.0)
              for h in heads]
    for level in range(GLA_LEVELS):
        for h in heads:
            f = jnp.exp(-jnp.abs(Ls[h] - _level_ref_rows(Ls[h], level)))
            s_level = _dot_nt((qs[h] * f).astype(BF16), (ks[h] * f).astype(BF16))
            scores[h] = jnp.where(pair_level == level, s_level, scores[h])
        tick()

    os = [os[h] + _dot(scores[h].astype(BF16), vs[h]) for h in heads]
    return os, new_states_t


def _gla_scan_tile(q_ref, k_ref, v_ref, r_ref, l_ref, g_ref, lvl_ref, o_ref, state_ref, tick):
    g = g_ref[0]
    pair_level = lvl_ref[...]

    for c in range(GLA_ROWS // CHUNK):
        rows = slice(c * CHUNK, (c + 1) * CHUNK)
        kcs = [slice(h * GLA_DK, (h + 1) * GLA_DK) for h in range(GLA_HEADS)]
        vcs = [slice(h * GLA_DV, (h + 1) * GLA_DV) for h in range(GLA_HEADS)]
        os, new_states = _gla_chunk([q_ref[rows, kc] for kc in kcs], [k_ref[rows, kc] for kc in kcs],
                                    [v_ref[rows, vc] for vc in vcs], [l_ref[rows, kc] for kc in kcs],
                                    [state_ref[h] for h in range(GLA_HEADS)], pair_level, tick)
        for h in range(GLA_HEADS):
            state_ref[h] = new_states[h]
            o_ref[rows, vcs[h]] = (_rms_norm(os[h], g) * r_ref[rows, vcs[h]]).astype(BF16)
        tick()


def _gla_front_kernel(tiles_per_seq, n_casts, x_ref, wqkvr_ref, wlr_ref, wgu_ref, bg_ref, tri_ref, g_ref,
                      lvl_ref, *rest):
    cast_src, o_ref, cast_dst = rest[:n_casts], rest[n_casts], rest[n_casts + 1:2 * n_casts + 1]
    state_ref, xb_ref = rest[2 * n_casts + 1:2 * n_casts + 3]
    slots = rest[2 * n_casts + 3:]
    step = pl.program_id(0)
    slot_a, slot_b = slots[:5], slots[5:]
    _cast_slabs(cast_src, cast_dst)

    @pl.when(step == 0)
    def _():
        for ref in slot_b:
            ref[...] = jnp.zeros_like(ref)

    @pl.when((step == 0) | (jax.lax.rem(step - 1, tiles_per_seq) == 0))
    def _():
        state_ref[...] = jnp.zeros_like(state_ref)

    def body(dst, src):
        items = _gla_project_items(x_ref, xb_ref, wqkvr_ref, wlr_ref, wgu_ref, bg_ref, tri_ref, *dst)
        n_items, n_ticks = len(items), (GLA_ROWS // CHUNK) * (GLA_LEVELS + 2)
        done = [0, 0]

        def tick():
            done[0] += 1
            while (done[1] < n_items and (done[1] - GLA_ITEMS_UP_FRONT) * (n_ticks - 2)
                   < done[0] * (n_items - GLA_ITEMS_UP_FRONT)):
                items[done[1]]()
                done[1] += 1

        for item in items[:GLA_ITEMS_UP_FRONT]:
            item()
        done[1] = GLA_ITEMS_UP_FRONT
        _gla_scan_tile(*src, g_ref, lvl_ref, o_ref, state_ref, tick)
        assert done == [n_ticks, n_items]

    parity = jax.lax.rem(step, 2)
    pl.when(parity == 0)(lambda: body(slot_a, slot_b))
    pl.when(parity == 1)(lambda: body(slot_b, slot_a))


def _gla_w_in_kernel(w_ref, main_ref, lr_ref):
    main = main_ref.shape[-1]
    w = w_ref[0]
    main_ref[0] = w[:, :main].astype(BF16)
    pad = jnp.zeros((w.shape[0], LANES - GLA_GATE_RANK), F32)
    lr_ref[0] = jnp.concatenate([w[:, main:], pad], axis=1).astype(BF16)


def _gla_weights(w_in, w_gate_up, b_gate, norm_g):
    layers, d, _ = w_in.shape
    main = 2 * GLA_HK + GLA_HV + D_MODEL
    rows = GLA_W_IN_ROWS
    w_main, wlr = pl.pallas_call(
        _gla_w_in_kernel,
        grid=(layers, d // rows),
        in_specs=[pl.BlockSpec((1, rows, w_in.shape[2]), lambda l, i: (l, i, 0))],
        out_specs=[pl.BlockSpec((1, rows, main), lambda l, i: (l, i, 0)),
                   pl.BlockSpec((1, rows, LANES), lambda l, i: (l, i, 0))],
        out_shape=[jax.ShapeDtypeStruct((layers, d, main), BF16), jax.ShapeDtypeStruct((layers, d, LANES), BF16)],
        compiler_params=_params("parallel", "parallel"),
        name="gla_w_in_cast",
    )(w_in)
    wgu = jnp.pad(w_gate_up, ((0, 0), (0, LANES - GLA_GATE_RANK), (0, 0))).astype(BF16)
    return w_main, (wlr, wgu, b_gate[:, None, :], norm_g[:, None, :])


def _gla_front(x, w_in, small_weights, layer, batch, seq, casts):
    n = x.shape[0]
    t = GLA_ROWS
    n_tiles = n // t
    wlr, wgu, bg, g = small_weights
    idx = np.arange(TRI_ROWS)
    tri = ((idx[:, None] // CHUNK == idx[None, :] // CHUNK) & (idx[None, :] <= idx[:, None]))
    tri = jnp.asarray(tri, BF16)
    idx = np.arange(CHUNK)
    differ = idx[:, None] ^ idx[None, :]
    pair_level = jnp.asarray(np.floor(np.log2(np.maximum(differ, 1))).astype(np.int32) - (differ == 0))
    slot = [pltpu.VMEM((t, GLA_HK), F32), pltpu.VMEM((t, GLA_HK), F32), pltpu.VMEM((t, GLA_HV), BF16),
            pltpu.VMEM((t, GLA_HV), F32), pltpu.VMEM((t, GLA_HK), F32)]
    cast_in, cast_out, cast_shapes = _cast_specs(casts, n_tiles)
    outs = pl.pallas_call(
        functools.partial(_gla_front_kernel, seq // t, len(casts)),
        grid=(n_tiles + 1,),
        in_specs=[pl.BlockSpec((t, D_MODEL), lambda s: (jnp.minimum(s, n_tiles - 1), 0)),
                  _layer_spec(*w_in), _layer_spec(wlr, layer), _layer_spec(wgu, layer),
                  _layer_spec(bg, layer), _const_spec(tri.shape), _layer_spec(g, layer),
                  _const_spec(pair_level.shape)] + cast_in,
        out_specs=[pl.BlockSpec((t, GLA_HV), lambda s: (jnp.maximum(s - 1, 0), 0))] + cast_out,
        out_shape=[jax.ShapeDtypeStruct((n, GLA_HV), BF16)] + cast_shapes,
        scratch_shapes=[pltpu.VMEM((GLA_HEADS, GLA_DV, GLA_DK), F32), pltpu.VMEM((t, D_MODEL), BF16)]
                       + slot + slot,
        compiler_params=_params("arbitrary"),
        name="gla_front",
    )(x, w_in[0], wlr, wgu, bg, tri, g, pair_level, *[s for s, _ in casts])
    return outs[0], outs[1:]


def _rope_table_kernel(pos_ref, freq_ref, cos_ref, sin_ref):
    ang = pos_ref[...] * freq_ref[...]
    cos_ref[...] = jnp.cos(ang)
    sin_ref[...] = jnp.sin(ang)


def _rope_tables(pos):
    n, half = pos.shape[0], MLA_ROPE // 2
    per_row = LANES // half
    inv_freq = ROPE_BASE ** (-jnp.arange(0, half, dtype=F32) * (2.0 / MLA_ROPE))
    pos_t = pos.reshape(n // MLA_PROJ_ROWS, per_row, MLA_PROJ_ROWS // per_row).transpose(0, 2, 1)
    pos_d = jnp.repeat(pos_t.reshape(n // per_row, per_row), half, axis=1)
    freq_d = jnp.tile(inv_freq, per_row)[None, :]
    rows_d = n // per_row
    tm = min(rows_d, ROPE_TABLE_ROWS)
    blk = pl.BlockSpec((tm, LANES), lambda i: (i, 0))
    cos_d, sin_d = pl.pallas_call(
        _rope_table_kernel,
        grid=(rows_d // tm,),
        in_specs=[blk, _const_spec(freq_d.shape)],
        out_specs=[blk, blk],
        out_shape=[jax.ShapeDtypeStruct((rows_d, LANES), F32)] * 2,
        compiler_params=_params("parallel"),
        name="rope_tables",
    )(pos_d, freq_d)
    return cos_d, sin_d


def _mla_proj_kernel(x_ref, cos_ref, sin_ref, win_ref, qn_ref, kvn_ref,
                     wq_ref, wk_ref, wvt_ref, q_ref, k_ref, vt_ref):
    H, R = MLA_HEADS, MLA_Q_RANK
    half = MLA_ROPE // 2
    group_rows = MLA_PROJ_ROWS // (LANES // half)
    scale = (MLA_NOPE + MLA_ROPE) ** -0.5

    def stages(r0):
        r1 = r0 + MLA_PROJ_SUB_ROWS
        c = _dot(x_ref[r0:r1, :].astype(BF16), win_ref[...])
        yield
        cq = _rms_norm(c[:, 0:R], qn_ref[...]).astype(BF16)
        ckv = _rms_norm(c[:, R:R + MLA_KV_RANK], kvn_ref[...]).astype(BF16)
        pad = jnp.zeros((group_rows, LANES - MLA_ROPE), F32)
        cos_blocks, sin_blocks = [], []
        for j in range(r0 // group_rows, r1 // group_rows):
            c32 = cos_ref[:, j * half:(j + 1) * half]
            s32 = sin_ref[:, j * half:(j + 1) * half]
            cos_blocks.append(jnp.concatenate([c32, c32, pad], axis=1))
            sin_blocks.append(jnp.concatenate([-s32, s32, pad], axis=1))
        cos = jnp.concatenate(cos_blocks, axis=0)
        sin = jnp.concatenate(sin_blocks, axis=0)
        k_rope = (c[:, 2 * R:2 * R + LANES] * cos + c[:, 2 * R + LANES:2 * R + 2 * LANES] * sin).astype(BF16)
        yield
        qq = _dot(cq, wq_ref[...])
        k_nope = _dot(ckv, wk_ref[...])
        vt_ref[:, r0:r1] = _dot_nt(wvt_ref[...], ckv).astype(BF16)
        yield
        for h in range(H):
            lo = h * LANES
            q_nope = qq[:, lo:lo + LANES]
            q_r = qq[:, H * LANES + lo:H * LANES + lo + LANES]
            q_sw = qq[:, 2 * H * LANES + lo:2 * H * LANES + lo + LANES]
            base = h * MLA_QK_PAD
            q_ref[r0:r1, base:base + LANES] = (q_nope * scale).astype(BF16)
            q_ref[r0:r1, base + LANES:base + 2 * LANES] = ((q_r * cos + q_sw * sin) * scale).astype(BF16)
            k_ref[r0:r1, base:base + LANES] = k_nope[:, lo:lo + LANES].astype(BF16)
            k_ref[r0:r1, base + LANES:base + 2 * LANES] = k_rope

    _skewed([stages(r0) for r0 in range(0, MLA_PROJ_ROWS, MLA_PROJ_SUB_ROWS)])


def _mla_proj(x, cos, sin, win, qn, kvn, wq, wk, wvt, seq):
    n = x.shape[0]
    tm = MLA_PROJ_ROWS
    per_b = seq // tm
    rows = lambda w: pl.BlockSpec((tm, w), lambda i: (i, 0))
    qk_w = MLA_HEADS * MLA_QK_PAD
    hv = MLA_HEADS * MLA_V
    table = pl.BlockSpec((tm // (LANES // (MLA_ROPE // 2)), LANES), lambda i: (i, 0))
    return pl.pallas_call(
        _mla_proj_kernel,
        grid=(n // tm,),
        in_specs=[rows(D_MODEL), table, table, _const_spec(win.shape), _const_spec(qn.shape), _const_spec(kvn.shape),
                  _const_spec(wq.shape), _const_spec(wk.shape), _const_spec(wvt.shape)],
        out_specs=[rows(qk_w), rows(qk_w),
                   pl.BlockSpec((hv, tm), lambda i: (i // per_b, i % per_b))],
        out_shape=[jax.ShapeDtypeStruct((n, qk_w), BF16), jax.ShapeDtypeStruct((n, qk_w), BF16),
                   jax.ShapeDtypeStruct((n // seq * hv, seq), BF16)],
        compiler_params=_params("parallel"),
        name="mla_proj",
    )(x, cos, sin, win, qn, kvn, wq, wk, wvt)


def _mla_attn_kernel(q_ref, k_ref, vt_ref, o_ref):
    seq = q_ref.shape[0]
    tq = MLA_Q_TILE
    key = jax.lax.broadcasted_iota(jnp.int32, (tq, tq), 0)
    qry = jax.lax.broadcasted_iota(jnp.int32, (tq, tq), 1)
    visible = jnp.right_shift(key, CHUNK_SHIFT) <= jnp.right_shift(qry, CHUNK_SHIFT)

    def scores(h, qi):
        lo, hi = qi * tq, (qi + 1) * tq
        mid = hi // 2
        qk = slice(h * MLA_QK_PAD, (h + 1) * MLA_QK_PAD)
        q = q_ref[lo:hi, qk]
        parts = []
        for k0, k1 in ((0, mid), (mid, hi)):
            s = _dot_nt(k_ref[k0:k1, qk], q)
            m0 = max(k0, lo)
            if m0 < k1:
                masked = jnp.where(visible[m0 - lo:k1 - lo, :], s[m0 - k0:, :], -jnp.inf)
                s = masked if m0 == k0 else jnp.concatenate([s[:m0 - k0, :], masked], axis=0)
            parts.append((k0, k1, s))
        return parts

    def finish(h, qi, parts):
        lo, hi = qi * tq, (qi + 1) * tq
        dv = slice(h * MLA_V, (h + 1) * MLA_V)
        m = functools.reduce(jnp.maximum, [jnp.max(s, axis=0, keepdims=True) for _, _, s in parts])
        den, acc = 0.0, 0.0
        for k0, k1, s in parts:
            p = jnp.exp(s - m)
            den = den + jnp.sum(p, axis=0, keepdims=True)
            acc = acc + _dot(vt_ref[dv, k0:k1], p.astype(BF16))
        o_ref[lo:hi, dv] = jnp.transpose(acc / den).astype(BF16)

    units = [(h, qi) for qi in reversed(range(seq // tq)) for h in range(MLA_HEADS_PER_STEP)]
    ahead = scores(*units[0])
    for i, unit in enumerate(units):
        current = ahead
        if i + 1 < len(units):
            ahead = scores(*units[i + 1])
        finish(*unit, current)


def _mla_attn(q, k, vt, batch, seq):
    n = q.shape[0]
    hs = MLA_HEADS_PER_STEP
    blk = lambda w: pl.BlockSpec((seq, hs * w), lambda b, h: (b, h))
    return pl.pallas_call(
        _mla_attn_kernel,
        grid=(batch, MLA_HEADS // hs),
        in_specs=[blk(MLA_QK_PAD), blk(MLA_QK_PAD),
                  pl.BlockSpec((hs * MLA_V, seq), lambda b, h: (b * (MLA_HEADS // hs) + h, 0))],
        out_specs=blk(MLA_V),
        out_shape=jax.ShapeDtypeStruct((n, MLA_HEADS * MLA_V), BF16),
        compiler_params=_params("parallel", "parallel"),
        name="mla_attn",
    )(q, k, vt)


def _mla_front(x, pos, w_in, q_norm, kv_norm, w_uq, w_ukv, batch, seq):
    H, half = MLA_HEADS, MLA_ROPE // 2
    pad = LANES - MLA_ROPE

    def rope_cols(w, swap):
        a, b = w[..., :half], w[..., half:]
        if swap:
            a, b = b, a
        return jnp.concatenate([a, b, jnp.zeros(w.shape[:-1] + (pad,), w.dtype)], axis=-1)

    kr_w = w_in[:, MLA_Q_RANK + MLA_KV_RANK:]
    win = jnp.concatenate([w_in[:, :MLA_Q_RANK + MLA_KV_RANK], rope_cols(kr_w, False),
                           rope_cols(kr_w, True)], axis=1).astype(BF16)
    wq3 = w_uq.reshape(MLA_Q_RANK, H, MLA_NOPE + MLA_ROPE)
    wq = jnp.concatenate([wq3[:, :, :MLA_NOPE].reshape(MLA_Q_RANK, H * LANES),
                          rope_cols(wq3[:, :, MLA_NOPE:], False).reshape(MLA_Q_RANK, H * LANES),
                          rope_cols(wq3[:, :, MLA_NOPE:], True).reshape(MLA_Q_RANK, H * LANES)],
                         axis=1).astype(BF16)
    wkv3 = w_ukv.reshape(MLA_KV_RANK, H, MLA_NOPE + MLA_V)
    wk = wkv3[:, :, :MLA_NOPE].reshape(MLA_KV_RANK, H * MLA_NOPE).astype(BF16)
    wvt = wkv3[:, :, MLA_NOPE:].reshape(MLA_KV_RANK, H * MLA_V).T.astype(BF16)
    cos, sin = _rope_tables(pos)
    q, k, v = _mla_proj(x, cos, sin, win, q_norm[None, :], kv_norm[None, :], wq, wk, wvt, seq)
    return _mla_attn(q, k, v, batch, seq)


def _conv_kernel(x_ref, win_ref, cw_ref, o_ref, cu_ref, xb_ref):
    ts = x_ref.shape[0]
    halo = SUBLANES

    @pl.when(pl.program_id(1) == 0)
    def _():
        cu_ref[0:halo, :] = jnp.zeros((halo, D_MODEL), F32)

    cw = cw_ref[0]

    xb_ref[...] = x_ref[...].astype(BF16)

    def stages(c0):
        cols = slice(c0, c0 + MXU_WIDTH)
        xb = xb_ref[...]
        b = _dot(xb, win_ref[0, :, c0:c0 + MXU_WIDTH])
        cu = (_dot(xb, win_ref[0, :, D_MODEL + c0:D_MODEL + c0 + MXU_WIDTH])
              * _dot(xb, win_ref[0, :, 2 * D_MODEL + c0:2 * D_MODEL + c0 + MXU_WIDTH]))
        cu_ref[halo:halo + ts, cols] = cu
        yield
        z = (cw[0:1, cols] * cu_ref[halo - 2:halo - 2 + ts, cols]
             + cw[1:2, cols] * cu_ref[halo - 1:halo - 1 + ts, cols]
             + cw[2:3, cols] * cu)
        o_ref[:, cols] = (b * z).astype(BF16)
        cu_ref[0:halo, cols] = cu_ref[ts:ts + halo, cols]

    _skewed([stages(c0) for c0 in range(0, D_MODEL, MXU_WIDTH)])


def _conv_front(x, w_in, conv_w_all, layer, batch, seq):
    n = x.shape[0]
    ts = CONV_ROWS
    per_b = seq // ts
    rows = lambda w: pl.BlockSpec((ts, w), lambda b, j: (b * per_b + j, 0))
    return pl.pallas_call(
        _conv_kernel,
        grid=(batch, per_b),
        in_specs=[rows(D_MODEL), _layer_spec(*w_in), _layer_spec(conv_w_all, layer)],
        out_specs=rows(D_MODEL),
        out_shape=jax.ShapeDtypeStruct((n, D_MODEL), BF16),
        scratch_shapes=[pltpu.VMEM((SUBLANES + ts, D_MODEL), F32), pltpu.VMEM((ts, D_MODEL), BF16)],
        compiler_params=_params("parallel", "arbitrary"),
        name="conv_front",
    )(x, w_in[0], conv_w_all)


def kernel(x, p, positions, gla_w_in, gla_w_gate_up, gla_b_gate, gla_norm_g, gla_w_out,
           mla_w_in, mla_q_norm, mla_kv_norm, mla_w_uq, mla_w_ukv, mla_w_out,
           conv_w_in, conv_w, conv_w_out, ln_g, ln_b, mlp_w1, mlp_w2, ple_w_gate, ple_w_proj):
    batch, seq, d = x.shape
    n = batch * seq
    assert d == D_MODEL and seq % GLA_ROWS == 0 and seq % CONV_ROWS == 0 and seq % MLA_Q_TILE == 0
    xf = x.reshape(n, d)
    pos = positions.astype(F32).reshape(n)
    p_all = p.reshape(DEPTH * n, PLE_DIM)
    wp_all = ple_w_proj.astype(BF16)
    gla_main, gla_small = _gla_weights(gla_w_in, gla_w_gate_up, gla_b_gate, gla_norm_g)
    w_out_f32 = (gla_w_out, mla_w_out, conv_w_out)
    mixer_w_in_f32 = (None, None, conv_w_in)

    tail_weights = lambda layer, kind, j: [(w_out_f32[kind], j), (mlp_w1, layer), (mlp_w2, layer),
                                           (ple_w_gate, layer)]
    mixer_w_in = None
    for i in range(DEPTH):
        j, kind = i // N_MIXERS, i % N_MIXERS
        if kind == 0:
            o, cast = _gla_front(xf, (gla_main, j), gla_small, j, batch, seq,
                                 tail_weights(0, kind, j) if i == 0 else [])
            if i == 0:
                wout, w1, w2, wg = [(c, 0) for c in cast]
        elif kind == 1:
            o = _mla_front(xf, pos, mla_w_in[j], mla_q_norm[j], mla_kv_norm[j], mla_w_uq[j],
                           mla_w_ukv[j], batch, seq)
        else:
            o = _conv_front(xf, mixer_w_in, conv_w, j, batch, seq)
        casts = []
        if i + 1 < DEPTH:
            j_next, kind_next = (i + 1) // N_MIXERS, (i + 1) % N_MIXERS
            casts = tail_weights(i + 1, kind_next, j_next)
            if mixer_w_in_f32[kind_next] is not None:
                casts.append((mixer_w_in_f32[kind_next], j_next))
        xf, cast = _tail(o, xf, p_all, i, wout, w1, w2, wg, wp_all, ln_g, ln_b, casts)
        if cast:
            wout, w1, w2, wg = [(c, 0) for c in cast[:4]]
            mixer_w_in = (cast[4], 0) if len(cast) > 4 else None
    return xf.reshape(batch, seq, d)
```
